```python
import jax
import jax.numpy as jnp
from jax import lax
import numpy as np

D_MODEL = 2048
BATCH = 1
SEQ = 16384
DEPTH = 2

GRID_W = 64
CTX_LEN = 256
N_MOD = 9
D_FF = 5632
MLA_HEADS = 8
MLA_Q_LORA = 512
MLA_KV_LORA = 512
MLA_NOPE = 128
MLA_ROPE = 64
MLA_VDIM = 128
GQA_HEADS = 8
GQA_KV_HEADS = 2
GQA_HEAD_DIM = 128
RET_HEADS = 4
RET_DK = 128
RET_DV = 256
RET_CHUNK = 128
N_BRANCH = 3
Q_BLOCK = 128
ROPE_THETA = 10000.0
NORM_EPS = 1e-6
MLA_SCALE = (MLA_NOPE + MLA_ROPE) ** -0.5
GQA_SCALE = GQA_HEAD_DIM ** -0.5
IN_SIZES = (MLA_Q_LORA, MLA_KV_LORA, MLA_ROPE,
            GQA_HEADS * GQA_HEAD_DIM, GQA_KV_HEADS * GQA_HEAD_DIM, GQA_KV_HEADS * GQA_HEAD_DIM,
            RET_HEADS * RET_DK, RET_HEADS * RET_DK, RET_HEADS * RET_DV, RET_HEADS * RET_DV,
            N_BRANCH * D_MODEL)
IN_COLS = sum(IN_SIZES)

kernel_name = 'hybrid_mla_gqa_retention_macaron_dit'


def rms_norm(x, gain=None):
    xf = x.astype(jnp.float32)
    y = xf * lax.rsqrt(jnp.mean(xf * xf, axis=-1, keepdims=True) + NORM_EPS)
    if gain is not None:
        y = y * gain.astype(jnp.float32)
    return y.astype(x.dtype)


def modulate(h, shift, scale):
    return h * (1.0 + scale) + shift


def swiglu(h, w_in, w_out):
    a, b = jnp.split(h @ w_in, 2, axis=-1)
    return (jax.nn.silu(a) * b) @ w_out


def adaln(cvec, w, b):
    m = jax.nn.silu(cvec) @ w + b
    return m.reshape(cvec.shape[0], 1, N_MOD, -1)


def axial_rope_table(rows, cols, dim):
    d_axis = dim // 2
    inv_freq = ROPE_THETA ** (-jnp.arange(0, d_axis, 2, dtype=jnp.float32) / d_axis)
    ang = jnp.concatenate([rows[:, None] * inv_freq, cols[:, None] * inv_freq], axis=-1)
    return jnp.cos(ang), jnp.sin(ang)


def apply_rope(x, rope):
    cos, sin = rope
    cos = cos[None, :, None, :].astype(x.dtype)
    sin = sin[None, :, None, :].astype(x.dtype)
    half = x.shape[-1] // 2
    x1, x2 = x[..., :half], x[..., half:]
    return jnp.concatenate([x1 * cos - x2 * sin, x1 * sin + x2 * cos], axis=-1)


def flip_seq(a):
    return jnp.flip(a, axis=1)


def block_attention(q, k, v, scale):
    b, lq, h, dq = q.shape
    hk, dv = k.shape[2], v.shape[-1]
    g = h // hk
    nb = lq // Q_BLOCK
    qb = q.reshape(b, nb, Q_BLOCK, hk, g, dq).transpose(1, 0, 2, 3, 4, 5)

    def one_block(q_blk):
        s = jnp.einsum('bqkgd,bskd->bkgqs', q_blk, k, preferred_element_type=jnp.float32) * scale
        p = jax.nn.softmax(s, axis=-1).astype(v.dtype)
        return jnp.einsum('bkgqs,bskd->bqkgd', p, v)

    o = lax.map(one_block, qb)
    return o.transpose(1, 0, 2, 3, 4, 5).reshape(b, lq, h, dv)


def retention_chunkwise(q, k, v, log_gamma, s0):
    b, l, h, dk = q.shape
    dv = v.shape[-1]
    n = l // RET_CHUNK

    def chunks(a):
        return a.astype(jnp.float32).reshape(b, n, RET_CHUNK, h, a.shape[-1]).transpose(1, 0, 3, 2, 4)

    lg = log_gamma.astype(jnp.float32)
    idx = jnp.arange(RET_CHUNK, dtype=jnp.float32)
    rel = idx[:, None] - idx[None, :]
    decay = jnp.where(rel >= 0, jnp.exp(lg[:, None, None] * jnp.maximum(rel, 0.0)), 0.0)
    q_decay = jnp.exp(lg[:, None] * (idx + 1.0))[:, :, None]
    k_decay = jnp.exp(lg[:, None] * (RET_CHUNK - 1.0 - idx))[:, :, None]
    c_decay = jnp.exp(lg * RET_CHUNK)[:, None, None]

    def step(s, inp):
        qc, kc, vc = inp
        att = jnp.einsum('bhid,bhjd->bhij', qc, kc) * decay
        o = jnp.einsum('bhij,bhje->bhie', att, vc) + jnp.einsum('bhid,bhde->bhie', qc * q_decay, s)
        s = c_decay * s + jnp.einsum('bhjd,bhje->bhde', kc * k_decay, vc)
        return s, o

    s_fin, o = lax.scan(step, s0, (chunks(q), chunks(k), chunks(v)))
    return o.transpose(1, 0, 3, 2, 4).reshape(b, l, h, dv), s_fin


def split_in(z):
    bounds = np.cumsum(IN_SIZES)[:-1].tolist()
    return jnp.split(z, bounds, axis=-1)


def mla_qkv(z_cq, z_ckv, z_kr, p, rope):
    b, l, _ = z_cq.shape
    q = (rms_norm(z_cq, p['mla_q_norm']) @ p['mla_w_uq']).reshape(b, l, MLA_HEADS, MLA_NOPE + MLA_ROPE)
    kv = (rms_norm(z_ckv, p['mla_kv_norm']) @ p['mla_w_ukv']).reshape(b, l, MLA_HEADS, MLA_NOPE + MLA_VDIM)
    q_nope, q_rope = q[..., :MLA_NOPE], q[..., MLA_NOPE:]
    k_nope, v = kv[..., :MLA_NOPE], kv[..., MLA_NOPE:]
    k_rope = z_kr[:, :, None, :]
    if rope is not None:
        q_rope = apply_rope(q_rope, rope)
        k_rope = apply_rope(k_rope, rope)
    k_rope = jnp.broadcast_to(k_rope, (b, l, MLA_HEADS, MLA_ROPE))
    q = jnp.concatenate([q_nope, q_rope], axis=-1)
    k = jnp.concatenate([k_nope, k_rope], axis=-1)
    return q, k, v


def gqa_qkv(z_q, z_k, z_v, p, rope):
    b, l, _ = z_q.shape
    q = rms_norm(z_q.reshape(b, l, GQA_HEADS, GQA_HEAD_DIM), p['gqa_q_norm'])
    k = rms_norm(z_k.reshape(b, l, GQA_KV_HEADS, GQA_HEAD_DIM), p['gqa_k_norm'])
    v = z_v.reshape(b, l, GQA_KV_HEADS, GQA_HEAD_DIM)
    if rope is not None:
        q = apply_rope(q, rope)
        k = apply_rope(k, rope)
    return q, k, v


def ret_qkv(z_q, z_k, z_v, rope):
    b, l, _ = z_q.shape
    q = z_q.reshape(b, l, RET_HEADS, RET_DK)
    k = z_k.reshape(b, l, RET_HEADS, RET_DK) * (RET_DK ** -0.5)
    v = z_v.reshape(b, l, RET_HEADS, RET_DV)
    if rope is not None:
        q = apply_rope(q, rope)
        k = apply_rope(k, rope)
    return q, k, v


def retention_output(o, z_g):
    b, l = o.shape[:2]
    y = rms_norm(o).reshape(b, l, RET_HEADS * RET_DV)
    return (y * jax.nn.silu(z_g.astype(jnp.float32))).astype(z_g.dtype)


def merge_branches(z_gate, o_mla, o_gqa, o_ret, p):
    b, l, _ = z_gate.shape
    gates = jax.nn.sigmoid(z_gate).reshape(b, l, N_BRANCH, -1)
    y = (gates[:, :, 0] * (o_mla.reshape(b, l, -1) @ p['branch_w_mla'])
         + gates[:, :, 1] * (o_gqa.reshape(b, l, -1) @ p['branch_w_gqa'])
         + gates[:, :, 2] * (o_ret @ p['branch_w_ret']))
    return y @ p['mix_w_out']


def token_mixer(h, hc, p, ropes, need_ctx):
    rope_mla, rope_gqa, rope_ret = ropes
    zl = split_in(h @ p['mix_w_in'])
    zc = split_in(hc @ p['mix_w_in'])
    qa_l, ka_l, va_l = mla_qkv(zl[0], zl[1], zl[2], p, rope_mla)
    qa_c, ka_c, va_c = mla_qkv(zc[0], zc[1], zc[2], p, None)
    oa_l = block_attention(qa_l, jnp.concatenate([ka_c, ka_l], axis=1),
                           jnp.concatenate([va_c, va_l], axis=1), MLA_SCALE)
    qb_l, kb_l, vb_l = gqa_qkv(zl[3], zl[4], zl[5], p, rope_gqa)
    qb_c, kb_c, vb_c = gqa_qkv(zc[3], zc[4], zc[5], p, None)
    ob_l = block_attention(qb_l, jnp.concatenate([kb_c, kb_l], axis=1),
                           jnp.concatenate([vb_c, vb_l], axis=1), GQA_SCALE)
    qr_l, kr_l, vr_l = ret_qkv(zl[6], zl[7], zl[8], rope_ret)
    qr_c, kr_c, vr_c = ret_qkv(zc[6], zc[7], zc[8], None)
    lg_fwd, lg_bwd = p['ret_log_decay'][0], p['ret_log_decay'][1]
    s0 = jnp.zeros((hc.shape[0], RET_HEADS, RET_DK, RET_DV), jnp.float32)
    or_cf, st_f = retention_chunkwise(qr_c, kr_c, vr_c, lg_fwd, s0)
    or_cb, st_b = retention_chunkwise(flip_seq(qr_c), flip_seq(kr_c), flip_seq(vr_c), lg_bwd, s0)
    or_lf, _ = retention_chunkwise(qr_l, kr_l, vr_l, lg_fwd, st_f)
    or_lb, _ = retention_chunkwise(flip_seq(qr_l), flip_seq(kr_l), flip_seq(vr_l), lg_bwd, st_b)
    or_l = retention_output(or_lf + flip_seq(or_lb), zl[9])
    out_l = merge_branches(zl[10], oa_l, ob_l, or_l, p)
    if not need_ctx:
        return out_l, None
    oa_c = block_attention(qa_c, ka_c, va_c, MLA_SCALE)
    ob_c = block_attention(qb_c, kb_c, vb_c, GQA_SCALE)
    or_c = retention_output(or_cf + flip_seq(or_cb), zc[9])
    out_c = merge_branches(zc[10], oa_c, ob_c, or_c, p)
    return out_l, out_c


def trunk_layer(x, xc, c, c_ctx, p, ropes, need_ctx):
    m = adaln(c, p['mod_w'], p['mod_b'])
    mc = adaln(c_ctx[None], p['mod_w'], p['mod_b'])
    x = x + 0.5 * m[:, :, 2] * swiglu(modulate(rms_norm(x, p['norm_ffn1']), m[:, :, 0], m[:, :, 1]),
                                      p['ffn1_w_in'], p['ffn1_w_out'])
    xc = xc + 0.5 * mc[:, :, 2] * swiglu(modulate(rms_norm(xc, p['norm_ffn1']), mc[:, :, 0], mc[:, :, 1]),
                                         p['ffn1_w_in'], p['ffn1_w_out'])
    h = modulate(rms_norm(x, p['norm_mix']), m[:, :, 3], m[:, :, 4])
    hc = modulate(rms_norm(xc, p['norm_mix']), mc[:, :, 3], mc[:, :, 4])
    o, oc = token_mixer(h, hc, p, ropes, need_ctx)
    x = x + m[:, :, 5] * o
    x = x + 0.5 * m[:, :, 8] * swiglu(modulate(rms_norm(x, p['norm_ffn2']), m[:, :, 6], m[:, :, 7]),
                                      p['ffn2_w_in'], p['ffn2_w_out'])
    if need_ctx:
        xc = xc + mc[:, :, 5] * oc
        xc = xc + 0.5 * mc[:, :, 8] * swiglu(modulate(rms_norm(xc, p['norm_ffn2']), mc[:, :, 6], mc[:, :, 7]),
                                             p['ffn2_w_in'], p['ffn2_w_out'])
    return x, xc


def setup_inputs(seed: int = 0) -> dict:
    key = jax.random.key(seed)
    k = jax.random.split(key, 26)
    D = D_MODEL
    L = DEPTH

    def nrm(kk, shape, scale):
        return scale * jax.random.normal(kk, shape, jnp.float32)

    def gain(kk, shape):
        return 1.0 + 0.05 * jax.random.normal(kk, shape, jnp.float32)

    base_decay = jnp.log(1.0 - 2.0 ** (-5.0 - jnp.arange(RET_HEADS, dtype=jnp.float32)))
    ret_log_decay = base_decay * jnp.exp(0.1 * jax.random.normal(k[17], (L, 2, RET_HEADS), jnp.float32))
    w_mla = MLA_HEADS * MLA_VDIM
    w_gqa = GQA_HEADS * GQA_HEAD_DIM
    w_ret = RET_HEADS * RET_DV
    return {
        'x': nrm(k[0], (BATCH, SEQ, D), 1.0),
        'c': nrm(k[1], (BATCH, D), 1.0),
        'ctx': nrm(k[2], (BATCH, CTX_LEN, D), 1.0),
        'c_ctx': nrm(k[3], (D,), 1.0),
        'mod_w': nrm(k[4], (L, D, N_MOD * D), 0.5 * D ** -0.5),
        'mod_b': nrm(k[5], (L, N_MOD * D), 0.02),
        'norm_ffn1': gain(k[6], (L, D)),
        'ffn1_w_in': nrm(k[7], (L, D, 2 * D_FF), D ** -0.5),
        'ffn1_w_out': nrm(k[8], (L, D_FF, D), D_FF ** -0.5),
        'norm_mix': gain(k[9], (L, D)),
        'mix_w_in': nrm(k[10], (L, D, IN_COLS), D ** -0.5),
        'mla_q_norm': gain(k[11], (L, MLA_Q_LORA)),
        'mla_w_uq': nrm(k[12], (L, MLA_Q_LORA, MLA_HEADS * (MLA_NOPE + MLA_ROPE)), MLA_Q_LORA ** -0.5),
        'mla_kv_norm': gain(k[13], (L, MLA_KV_LORA)),
        'mla_w_ukv': nrm(k[14], (L, MLA_KV_LORA, MLA_HEADS * (MLA_NOPE + MLA_VDIM)), MLA_KV_LORA ** -0.5),
        'gqa_q_norm': gain(k[15], (L, GQA_HEAD_DIM)),
        'gqa_k_norm': gain(k[16], (L, GQA_HEAD_DIM)),
        'ret_log_decay': ret_log_decay,
        'branch_w_mla': nrm(k[18], (L, w_mla, D), w_mla ** -0.5),
        'branch_w_gqa': nrm(k[19], (L, w_gqa, D), w_gqa ** -0.5),
        'branch_w_ret': nrm(k[20], (L, w_ret, D), w_ret ** -0.5),
        'mix_w_out': nrm(k[21], (L, D, D), D ** -0.5),
        'norm_ffn2': gain(k[22], (L, D)),
        'ffn2_w_in': nrm(k[23], (L, D, 2 * D_FF), D ** -0.5),
        'ffn2_w_out': nrm(k[24], (L, D_FF, D), D_FF ** -0.5),
        'final_norm': gain(k[25], (D,)),
    }


def reference(x, c, ctx, c_ctx, mod_w, mod_b, norm_ffn1, ffn1_w_in, ffn1_w_out, norm_mix, mix_w_in,
              mla_q_norm, mla_w_uq, mla_kv_norm, mla_w_ukv, gqa_q_norm, gqa_k_norm, ret_log_decay,
              branch_w_mla, branch_w_gqa, branch_w_ret, mix_w_out, norm_ffn2, ffn2_w_in, ffn2_w_out,
              final_norm):
    n_tok = x.shape[1]
    n_rows = n_tok // GRID_W
    rows = jnp.repeat(jnp.arange(n_rows, dtype=jnp.float32), GRID_W)
    cols = jnp.tile(jnp.arange(GRID_W, dtype=jnp.float32), n_rows)
    ropes = (axial_rope_table(rows, cols, MLA_ROPE),
             axial_rope_table(rows, cols, GQA_HEAD_DIM),
             axial_rope_table(rows, cols, RET_DK))
    xc = ctx
    for layer in range(DEPTH):
        p = {
            'mod_w': mod_w[layer], 'mod_b': mod_b[layer],
            'norm_ffn1': norm_ffn1[layer], 'ffn1_w_in': ffn1_w_in[layer], 'ffn1_w_out': ffn1_w_out[layer],
            'norm_mix': norm_mix[layer], 'mix_w_in': mix_w_in[layer],
            'mla_q_norm': mla_q_norm[layer], 'mla_w_uq': mla_w_uq[layer],
            'mla_kv_norm': mla_kv_norm[layer], 'mla_w_ukv': mla_w_ukv[layer],
            'gqa_q_norm': gqa_q_norm[layer], 'gqa_k_norm': gqa_k_norm[layer],
            'ret_log_decay': ret_log_decay[layer],
            'branch_w_mla': branch_w_mla[layer], 'branch_w_gqa': branch_w_gqa[layer],
            'branch_w_ret': branch_w_ret[layer], 'mix_w_out': mix_w_out[layer],
            'norm_ffn2': norm_ffn2[layer], 'ffn2_w_in': ffn2_w_in[layer], 'ffn2_w_out': ffn2_w_out[layer],
        }
        x, xc = trunk_layer(x, xc, c, c_ctx, p, ropes, need_ctx=layer < DEPTH - 1)
    return rms_norm(x, final_norm)
```

```python
import functools
import math

import numpy as np
import jax
import jax.numpy as jnp
from jax import lax
from jax.experimental import pallas as pl
from jax.experimental.pallas import tpu as pltpu

F32 = jnp.float32
BF16 = jnp.bfloat16

GRID_W = 64
N_MOD = 9
MLA_HEADS = 8
MLA_Q_LORA = 512
MLA_KV_LORA = 512
MLA_NOPE = 128
MLA_ROPE = 64
MLA_VDIM = 128
GQA_HEADS = 8
GQA_KV_HEADS = 2
GQA_HEAD_DIM = 128
RET_HEADS = 4
RET_DK = 128
RET_DV = 256
ROPE_THETA = 10000.0
NORM_EPS = 1e-6
LOG2E = math.log2(math.e)
MLA_QSCALE = (MLA_NOPE + MLA_ROPE) ** -0.5 * LOG2E
GQA_QSCALE = GQA_HEAD_DIM ** -0.5 * LOG2E
MLA_QK = 256

LANES = 128
V7X_VMEM_LIMIT = 56 * 1024 * 1024

Z_GATE0 = 0
Z_COLS = 12288


def _cparams(sem):
    return pltpu.CompilerParams(dimension_semantics=sem, vmem_limit_bytes=V7X_VMEM_LIMIT)


def _rms(x):
    return x * lax.rsqrt(jnp.mean(x * x, axis=-1, keepdims=True) + NORM_EPS)


def _dot(a, b):
    return jnp.dot(a, b, preferred_element_type=F32)


def _dot_nt(a, b):
    return lax.dot_general(a, b, (((1,), (1,)), ((), ())), preferred_element_type=F32)


def _modvec_kernel(cb_ref, w_ref, b_ref, o_ref):
    tn = w_ref.shape[1]
    d = w_ref.shape[0]
    for r in range(2):
        cb = cb_ref[r]
        s = cb * jax.nn.sigmoid(cb)
        for g in range(tn // LANES):
            w = w_ref[:, g * LANES:(g + 1) * LANES]
            part = jnp.sum((w * s).reshape(d // 8, 8, LANES), axis=0)
            o_ref[r:r + 1, g * LANES:(g + 1) * LANES] = (
                jnp.sum(part, axis=0, keepdims=True) + b_ref[:, g * LANES:(g + 1) * LANES])


def modvec(cvecs, w, b):
    d, n = w.shape
    tn = 1024
    cb = jnp.broadcast_to(cvecs[:, :, None], (2, d, LANES))
    return pl.pallas_call(
        _modvec_kernel,
        grid=(n // tn,),
        in_specs=[pl.BlockSpec((2, d, LANES), lambda j: (0, 0, 0)),
                  pl.BlockSpec((d, tn), lambda j: (0, j)),
                  pl.BlockSpec((1, tn), lambda j: (0, j))],
        out_specs=pl.BlockSpec((2, tn), lambda j: (0, j)),
        out_shape=jax.ShapeDtypeStruct((2, n), F32),
        compiler_params=_cparams(("arbitrary",)),
    )(cb, w, b.reshape(1, n))


def _ffn_kernel(x_ref, g_ref, mod_ref, wa_ref, wb_ref, wo_ref, fin_ref, o_ref, h_ref, *, final_norm):
    j = pl.program_id(1)

    @pl.when(j == 0)
    def _():
        y = _rms(x_ref[...]) * g_ref[...]
        h = y * (1.0 + mod_ref[1:2, :]) + mod_ref[0:1, :]
        h_ref[...] = h.astype(BF16)
        o_ref[...] = jnp.zeros_like(o_ref)

    h = h_ref[...]
    a = _dot(h, wa_ref[...])
    b = _dot(h, wb_ref[...])
    act = (a * jax.nn.sigmoid(a) * b).astype(BF16)
    o_ref[...] += _dot(act, wo_ref[...])

    @pl.when(j == pl.num_programs(1) - 1)
    def _():
        xn = x_ref[...] + (0.5 * mod_ref[2:3, :]) * o_ref[...]
        if final_norm:
            xn = _rms(xn) * fin_ref[...]
        o_ref[...] = xn


def ffn(x, gain, mod3, w_in, w_out, fin=None):
    m, d = x.shape
    f = w_out.shape[0]
    tm = min(512, m)
    tf = 512
    nf = f // tf
    final_norm = fin is not None
    if fin is None:
        fin = jnp.ones((d,), F32)
    return pl.pallas_call(
        functools.partial(_ffn_kernel, final_norm=final_norm),
        grid=(m // tm, nf),
        in_specs=[pl.BlockSpec((tm, d), lambda i, j: (i, 0)),
                  pl.BlockSpec((1, d), lambda i, j: (0, 0)),
                  pl.BlockSpec((3, d), lambda i, j: (0, 0)),
                  pl.BlockSpec((d, tf), lambda i, j: (0, j)),
                  pl.BlockSpec((d, tf), lambda i, j: (0, j + nf)),
                  pl.BlockSpec((tf, d), lambda i, j: (j, 0)),
                  pl.BlockSpec((1, d), lambda i, j: (0, 0))],
        out_specs=pl.BlockSpec((tm, d), lambda i, j: (i, 0)),
        out_shape=jax.ShapeDtypeStruct((m, d), F32),
        scratch_shapes=[pltpu.VMEM((tm, d), BF16)],
        compiler_params=_cparams(("parallel", "arbitrary")),
    )(x, gain.reshape(1, d), mod3, w_in, w_in, w_out, fin.reshape(1, d))


def _inproj_kernel(x_ref, g_ref, mod_ref, w_ref, o_ref, h_ref):
    @pl.when(pl.program_id(1) == 0)
    def _():
        y = _rms(x_ref[...]) * g_ref[...]
        h_ref[...] = (y * (1.0 + mod_ref[1:2, :]) + mod_ref[0:1, :]).astype(BF16)

    o_ref[...] = _dot(h_ref[...], w_ref[...]).astype(o_ref.dtype)


def inproj(x, gain, mod2, w):
    m, d = x.shape
    n = w.shape[1]
    tm = min(512, m)
    tn = 1536
    return pl.pallas_call(
        _inproj_kernel,
        grid=(m // tm, n // tn),
        in_specs=[pl.BlockSpec((tm, d), lambda i, j: (i, 0)),
                  pl.BlockSpec((1, d), lambda i, j: (0, 0)),
                  pl.BlockSpec((2, d), lambda i, j: (0, 0)),
                  pl.BlockSpec((d, tn), lambda i, j: (0, j))],
        out_specs=pl.BlockSpec((tm, tn), lambda i, j: (i, j)),
        out_shape=jax.ShapeDtypeStruct((m, n), BF16),
        scratch_shapes=[pltpu.VMEM((tm, d), BF16)],
        compiler_params=_cparams(("parallel", "arbitrary")),
    )(x, gain.reshape(1, d), mod2, w)


def _mla_prep_kernel(z_ref, gq_ref, gkv_ref, wqt_ref, wk_ref, wvt_ref, cost_ref, sint_ref,
                     kcos_ref, ksa_ref, ksb_ref, qt_ref, kc_ref, vt_ref):
    cq = z_ref[:, 0:MLA_Q_LORA].astype(F32)
    ckv = z_ref[:, MLA_Q_LORA:MLA_Q_LORA + MLA_KV_LORA].astype(F32)
    kr = z_ref[:, MLA_Q_LORA + MLA_KV_LORA:MLA_Q_LORA + MLA_KV_LORA + LANES].astype(F32)
    cqn = (_rms(cq) * gq_ref[...]).astype(BF16)
    ckvn = (_rms(ckv) * gkv_ref[...]).astype(BF16)
    kr_r = (kr * kcos_ref[...] + pltpu.roll(kr, 32, 1) * ksa_ref[...]
            + pltpu.roll(kr, 96, 1) * ksb_ref[...]).astype(BF16)
    knope = _dot(ckvn, wk_ref[...])
    vt = _dot_nt(wvt_ref[...], ckvn)
    cos_t = cost_ref[...]
    sin_t = sint_ref[...]
    half = MLA_ROPE // 2
    for h in range(MLA_HEADS):
        qt = _dot_nt(wqt_ref[h], cqn) * MLA_QSCALE
        x1 = qt[MLA_NOPE:MLA_NOPE + half]
        x2 = qt[MLA_NOPE + half:MLA_NOPE + 2 * half]
        qt_ref[h, 0:MLA_NOPE, :] = qt[0:MLA_NOPE].astype(BF16)
        qt_ref[h, MLA_NOPE:MLA_NOPE + half, :] = (x1 * cos_t - x2 * sin_t).astype(BF16)
        qt_ref[h, MLA_NOPE + half:MLA_NOPE + 2 * half, :] = (x1 * sin_t + x2 * cos_t).astype(BF16)
        qt_ref[h, MLA_NOPE + 2 * half:, :] = qt[MLA_NOPE + 2 * half:].astype(BF16)
        kc_ref[h, :, 0:MLA_NOPE] = knope[:, h * MLA_NOPE:(h + 1) * MLA_NOPE].astype(BF16)
        kc_ref[h, :, MLA_NOPE:] = kr_r
        vt_ref[h] = vt[h * MLA_VDIM:(h + 1) * MLA_VDIM].astype(BF16)


def mla_prep(z, zblk, gq, gkv, wqt, wk, wvt, rope):
    m = z.shape[0]
    tm = min(512, m)
    cos_t, sin_t, kcos, ksa, ksb = rope
    hh = MLA_HEADS
    return pl.pallas_call(
        _mla_prep_kernel,
        grid=(m // tm,),
        in_specs=[pl.BlockSpec((tm, 1536), lambda i: (i, zblk)),
                  pl.BlockSpec((1, MLA_Q_LORA), lambda i: (0, 0)),
                  pl.BlockSpec((1, MLA_KV_LORA), lambda i: (0, 0)),
                  pl.BlockSpec((hh, MLA_QK, MLA_Q_LORA), lambda i: (0, 0, 0)),
                  pl.BlockSpec((MLA_KV_LORA, hh * MLA_NOPE), lambda i: (0, 0)),
                  pl.BlockSpec((hh * MLA_VDIM, MLA_KV_LORA), lambda i: (0, 0)),
                  pl.BlockSpec((MLA_ROPE // 2, tm), lambda i: (0, i)),
                  pl.BlockSpec((MLA_ROPE // 2, tm), lambda i: (0, i)),
                  pl.BlockSpec((tm, LANES), lambda i: (i, 0)),
                  pl.BlockSpec((tm, LANES), lambda i: (i, 0)),
                  pl.BlockSpec((tm, LANES), lambda i: (i, 0))],
        out_specs=[pl.BlockSpec((hh, MLA_QK, tm), lambda i: (0, 0, i)),
                   pl.BlockSpec((hh, tm, MLA_QK), lambda i: (0, i, 0)),
                   pl.BlockSpec((hh, MLA_VDIM, tm), lambda i: (0, 0, i))],
        out_shape=[jax.ShapeDtypeStruct((hh, MLA_QK, m), BF16),
                   jax.ShapeDtypeStruct((hh, m, MLA_QK), BF16),
                   jax.ShapeDtypeStruct((hh, MLA_VDIM, m), BF16)],
        compiler_params=_cparams(("parallel",)),
    )(z, gq.reshape(1, -1), gkv.reshape(1, -1), wqt, wk, wvt, cos_t, sin_t, kcos, ksa, ksb)


def _rope128(x, cosf, sins):
    return x * cosf + pltpu.roll(x, 64, 1) * sins


def _gqa_prep_kernel(z_ref, gq_ref, gk_ref, cos_ref, sin_ref, qt_ref, k_ref, vt_ref):
    cosf = cos_ref[...]
    sins = sin_ref[...]
    dh = GQA_HEAD_DIM
    for h in range(GQA_HEADS):
        x = z_ref[:, h * dh:(h + 1) * dh].astype(F32)
        q = _rope128(_rms(x) * gq_ref[...], cosf, sins) * GQA_QSCALE
        qt_ref[h] = q.T.astype(BF16)
    k0 = GQA_HEADS * dh
    v0 = k0 + GQA_KV_HEADS * dh
    for h in range(GQA_KV_HEADS):
        x = z_ref[:, k0 + h * dh:k0 + (h + 1) * dh].astype(F32)
        k_ref[h] = _rope128(_rms(x) * gk_ref[...], cosf, sins).astype(BF16)
        v = z_ref[:, v0 + h * dh:v0 + (h + 1) * dh].astype(F32)
        vt_ref[h] = v.T.astype(BF16)


def gqa_prep(z, zblk, gq, gk, rope):
    m = z.shape[0]
    tm = min(512, m)
    cosf, sins = rope
    dh = GQA_HEAD_DIM
    return pl.pallas_call(
        _gqa_prep_kernel,
        grid=(m // tm,),
        in_specs=[pl.BlockSpec((tm, 1536), lambda i: (i, zblk)),
                  pl.BlockSpec((1, dh), lambda i: (0, 0)),
                  pl.BlockSpec((1, dh), lambda i: (0, 0)),
                  pl.BlockSpec((tm, dh), lambda i: (i, 0)),
                  pl.BlockSpec((tm, dh), lambda i: (i, 0))],
        out_specs=[pl.BlockSpec((GQA_HEADS, dh, tm), lambda i: (0, 0, i)),
                   pl.BlockSpec((GQA_KV_HEADS, tm, dh), lambda i: (0, i, 0)),
                   pl.BlockSpec((GQA_KV_HEADS, dh, tm), lambda i: (0, 0, i))],
        out_shape=[jax.ShapeDtypeStruct((GQA_HEADS, dh, m), BF16),
                   jax.ShapeDtypeStruct((GQA_KV_HEADS, m, dh), BF16),
                   jax.ShapeDtypeStruct((GQA_KV_HEADS, dh, m), BF16)],
        compiler_params=_cparams(("parallel",)),
    )(z, gq.reshape(1, dh), gk.reshape(1, dh), cosf, sins)


def _attn_kernel(*refs, tk, n_lat_chunks, dv):
    if n_lat_chunks:
        qt_ref, kl_ref, vlt_ref, kc_ref, vct_ref, o_ref, m_ref, l_ref, acc_ref = refs
    else:
        qt_ref, kc_ref, vct_ref, o_ref, m_ref, l_ref, acc_ref = refs
    qt = qt_ref[0]
    m_ref[...] = jnp.full_like(m_ref, -1e30)
    l_ref[...] = jnp.zeros_like(l_ref)
    acc_ref[...] = jnp.zeros_like(acc_ref)

    def update(k, vt):
        st = _dot(k, qt)
        m_old = m_ref[...]
        m_new = jnp.maximum(m_old, jnp.max(st, axis=0, keepdims=True))
        p = jnp.exp2(st - m_new)
        alpha = jnp.exp2(m_old - m_new)
        l_ref[...] = alpha * l_ref[...] + jnp.sum(p, axis=0, keepdims=True)
        acc_ref[...] = alpha * acc_ref[...] + _dot(vt, p.astype(BF16))
        m_ref[...] = m_new

    update(kc_ref[0], vct_ref[0])
    if n_lat_chunks:
        def body(c, carry):
            off = pl.multiple_of(c * tk, tk)
            update(kl_ref[0, pl.ds(off, tk), :], vlt_ref[0, :, pl.ds(off, tk)])
            return carry
        lax.fori_loop(0, n_lat_chunks, body, 0)

    o = acc_ref[...] / l_ref[...]
    o_ref[...] = o.T.astype(o_ref.dtype)


def attention(qt, k_lat, vt_lat, k_ctx, vt_ctx):
    hq, dq, lq = qt.shape
    hk, lc, _ = k_ctx.shape
    dv = vt_ctx.shape[1]
    grp = hq // hk
    tq = min(256, lq)
    if k_lat is not None:
        ll = k_lat.shape[1]
        tk = min(512, ll)
        n_chunks = ll // tk
    else:
        tk, n_chunks = 0, 0
    in_specs = [pl.BlockSpec((1, dq, tq), lambda h, i: (h, 0, i))]
    args = [qt]
    if k_lat is not None:
        in_specs += [pl.BlockSpec((1, ll, dq), lambda h, i: (h // grp, 0, 0)),
                     pl.BlockSpec((1, dv, ll), lambda h, i: (h // grp, 0, 0))]
        args += [k_lat, vt_lat]
    in_specs += [pl.BlockSpec((1, lc, dq), lambda h, i: (h // grp, 0, 0)),
                 pl.BlockSpec((1, dv, lc), lambda h, i: (h // grp, 0, 0))]
    args += [k_ctx, vt_ctx]
    return pl.pallas_call(
        functools.partial(_attn_kernel, tk=tk, n_lat_chunks=n_chunks, dv=dv),
        grid=(hq, lq // tq),
        in_specs=in_specs,
        out_specs=pl.BlockSpec((tq, dv), lambda h, i: (i, h)),
        out_shape=jax.ShapeDtypeStruct((lq, hq * dv), BF16),
        scratch_shapes=[pltpu.VMEM((1, tq), F32), pltpu.VMEM((1, tq), F32), pltpu.VMEM((dv, tq), F32)],
        compiler_params=_cparams(("parallel", "arbitrary")),
    )(*args)


def _ret_kernel(*refs, chunk, reverse, finalize):
    if finalize:
        (lg_ref, z_ref, cos_ref, sin_ref, s0_ref, of_ref,
         o_ref, sfin_ref, s_ref, dec_ref, qd_ref, kd_ref) = refs
    else:
        (lg_ref, z_ref, cos_ref, sin_ref, s0_ref,
         o_ref, sfin_ref, s_ref, dec_ref, qd_ref, kd_ref) = refs
        of_ref = None
    i = pl.program_id(0)
    c = chunk
    dk, dvv, nh = RET_DK, RET_DV, RET_HEADS

    @pl.when(i == 0)
    def _():
        s_ref[...] = s0_ref[...]
        row = lax.broadcasted_iota(jnp.int32, (c, c), 0).astype(F32)
        col = lax.broadcasted_iota(jnp.int32, (c, c), 1).astype(F32)
        rel = (col - row) if reverse else (row - col)
        pos = lax.broadcasted_iota(jnp.int32, (c, LANES), 0).astype(F32)
        if reverse:
            pos = (c - 1.0) - pos
        for h in range(nh):
            lg = lg_ref[h]
            dec_ref[h] = jnp.where(rel >= 0, jnp.exp(lg * jnp.maximum(rel, 0.0)), 0.0)
            qd_ref[h] = jnp.exp(lg * (pos + 1.0))
            kd_ref[h] = jnp.exp(lg * ((c - 1.0) - pos))

    cosf = cos_ref[...]
    sins = sin_ref[...]
    q0, k0, v0, g0 = 0, nh * dk, 2 * nh * dk, 2 * nh * dk + nh * dvv
    for h in range(nh):
        q = _rope128(z_ref[:, q0 + h * dk:q0 + (h + 1) * dk].astype(F32), cosf, sins)
        k = _rope128(z_ref[:, k0 + h * dk:k0 + (h + 1) * dk].astype(F32) * (dk ** -0.5), cosf, sins)
        v = z_ref[:, v0 + h * dvv:v0 + (h + 1) * dvv]
        s = s_ref[h]
        att = _dot_nt(q.astype(BF16), k.astype(BF16)) * dec_ref[h]
        o = _dot(att.astype(BF16), v) + _dot((q * qd_ref[h]).astype(BF16), s.astype(BF16))
        kdt = (k * kd_ref[h]).T.astype(BF16)
        cdec = jnp.exp(jnp.full((1, 1), c, F32) * lg_ref[h])
        s_ref[h] = cdec * s + _dot(kdt, v)
        if finalize:
            y = _rms(o + of_ref[:, h * dvv:(h + 1) * dvv])
            g = z_ref[:, g0 + h * dvv:g0 + (h + 1) * dvv].astype(F32)
            o_ref[:, h * dvv:(h + 1) * dvv] = (y * (g * jax.nn.sigmoid(g))).astype(o_ref.dtype)
        else:
            o_ref[:, h * dvv:(h + 1) * dvv] = o

    @pl.when(i == pl.num_programs(0) - 1)
    def _():
        sfin_ref[...] = s_ref[...]


def retention(z, zblk, lg, rope, s0, o_fwd=None):
    m = z.shape[0]
    c = min(256, m)
    n = m // c
    reverse = o_fwd is not None
    cosf, sins = rope
    nh, dk, dvv = RET_HEADS, RET_DK, RET_DV
    blk = (lambda i: (n - 1 - i, 0)) if reverse else (lambda i: (i, 0))
    zmap = (lambda i: (n - 1 - i, zblk)) if reverse else (lambda i: (i, zblk))
    in_specs = [pl.BlockSpec(memory_space=pltpu.SMEM),
                pl.BlockSpec((c, 3072), zmap),
                pl.BlockSpec((c, LANES), blk),
                pl.BlockSpec((c, LANES), blk),
                pl.BlockSpec((nh, dk, dvv), lambda i: (0, 0, 0))]
    args = [lg, z, cosf, sins, s0]
    if reverse:
        in_specs.append(pl.BlockSpec((c, nh * dvv), blk))
        args.append(o_fwd)
    return pl.pallas_call(
        functools.partial(_ret_kernel, chunk=c, reverse=reverse, finalize=reverse),
        grid=(n,),
        in_specs=in_specs,
        out_specs=[pl.BlockSpec((c, nh * dvv), blk),
                   pl.BlockSpec((nh, dk, dvv), lambda i: (0, 0, 0))],
        out_shape=[jax.ShapeDtypeStruct((m, nh * dvv), BF16 if reverse else F32),
                   jax.ShapeDtypeStruct((nh, dk, dvv), F32)],
        scratch_shapes=[pltpu.VMEM((nh, dk, dvv), F32), pltpu.VMEM((nh, c, c), F32),
                        pltpu.VMEM((nh, c, LANES), F32), pltpu.VMEM((nh, c, LANES), F32)],
        compiler_params=_cparams(("arbitrary",)),
    )(*args)


def _merge_kernel(x_ref, g0_ref, g1_ref, g2_ref, oa_ref, ob_ref, oc_ref, wa_ref, wb_ref, wc_ref,
                  wo_ref, mod_ref, o_ref):
    y = jax.nn.sigmoid(g0_ref[...].astype(F32)) * _dot(oa_ref[...], wa_ref[...])
    y += jax.nn.sigmoid(g1_ref[...].astype(F32)) * _dot(ob_ref[...], wb_ref[...])
    y += jax.nn.sigmoid(g2_ref[...].astype(F32)) * _dot(oc_ref[...], wc_ref[...])
    out = _dot(y.astype(BF16), wo_ref[...])
    o_ref[...] = x_ref[...] + mod_ref[...] * out


def merge(x, z, oa, ob, oc, wa, wb, wc, wo, gate):
    m, d = x.shape
    tm = min(256, m)
    const = lambda i: (0, 0)
    wspec = lambda w: pl.BlockSpec(w.shape, const, pipeline_mode=pl.Buffered(1))
    return pl.pallas_call(
        _merge_kernel,
        grid=(m // tm,),
        in_specs=[pl.BlockSpec((tm, d), lambda i: (i, 0)),
                  pl.BlockSpec((tm, d), lambda i: (i, 0)),
                  pl.BlockSpec((tm, d), lambda i: (i, 1)),
                  pl.BlockSpec((tm, d), lambda i: (i, 2)),
                  pl.BlockSpec((tm, oa.shape[1]), lambda i: (i, 0)),
                  pl.BlockSpec((tm, ob.shape[1]), lambda i: (i, 0)),
                  pl.BlockSpec((tm, oc.shape[1]), lambda i: (i, 0)),
                  wspec(wa), wspec(wb), wspec(wc), wspec(wo),
                  pl.BlockSpec((1, d), const)],
        out_specs=pl.BlockSpec((tm, d), lambda i: (i, 0)),
        out_shape=jax.ShapeDtypeStruct((m, d), F32),
        compiler_params=_cparams(("parallel",)),
    )(x, z, z, z, oa, ob, oc, wa, wb, wc, wo, gate.reshape(1, d))


def _rope_tables(n_tok, n_ctx):
    n_rows = n_tok // GRID_W
    rows = jnp.repeat(jnp.arange(n_rows, dtype=F32), GRID_W)
    cols = jnp.tile(jnp.arange(GRID_W, dtype=F32), n_rows)

    def table(dim):
        d_axis = dim // 2
        inv_freq = ROPE_THETA ** (-jnp.arange(0, d_axis, 2, dtype=F32) / d_axis)
        ang = jnp.concatenate([rows[:, None] * inv_freq, cols[:, None] * inv_freq], axis=-1)
        return jnp.cos(ang), jnp.sin(ang)

    def mla_pack(cos, sin):
        t = cos.shape[0]
        z32 = jnp.zeros((t, 32), F32)
        z64 = jnp.zeros((t, 64), F32)
        return (cos.T, sin.T,
                jnp.concatenate([cos, cos, z64], axis=1),
                jnp.concatenate([z32, sin, z64], axis=1),
                jnp.concatenate([-sin, z32, z64], axis=1))

    def pack128(cos, sin):
        return (jnp.concatenate([cos, cos], axis=1), jnp.concatenate([-sin, sin], axis=1))

    cm, sm = table(MLA_ROPE)
    c128, s128 = table(GQA_HEAD_DIM)
    lat = (mla_pack(cm, sm), pack128(c128, s128))
    one = lambda w: jnp.ones((n_ctx, w), F32)
    zero = lambda w: jnp.zeros((n_ctx, w), F32)
    ctx = (mla_pack(one(MLA_ROPE // 2), zero(MLA_ROPE // 2)), pack128(one(64), zero(64)))
    return lat, ctx


def _layer_weights(p, d):
    w = p['mix_w_in']
    sizes = (MLA_Q_LORA, MLA_KV_LORA, MLA_ROPE, GQA_HEADS * GQA_HEAD_DIM, GQA_KV_HEADS * GQA_HEAD_DIM,
             GQA_KV_HEADS * GQA_HEAD_DIM, RET_HEADS * RET_DK, RET_HEADS * RET_DK, RET_HEADS * RET_DV,
             RET_HEADS * RET_DV, 3 * d)
    offs = np.concatenate([[0], np.cumsum(sizes)])
    part = [w[:, offs[i]:offs[i + 1]] for i in range(len(sizes))]
    used = 3 * d + 3072 + 1536 + MLA_Q_LORA + MLA_KV_LORA + MLA_ROPE
    w_z = jnp.concatenate([part[10], part[6], part[7], part[8], part[9], part[3], part[4], part[5],
                           part[0], part[1], part[2], jnp.zeros((d, Z_COLS - used), w.dtype)], axis=1)
    hh = MLA_HEADS
    wq = p['mla_w_uq'].reshape(MLA_Q_LORA, hh, MLA_NOPE + MLA_ROPE)
    wq = jnp.concatenate([wq, jnp.zeros((MLA_Q_LORA, hh, MLA_QK - MLA_NOPE - MLA_ROPE), wq.dtype)], axis=2)
    wqt = wq.transpose(1, 2, 0)
    wkv = p['mla_w_ukv'].reshape(MLA_KV_LORA, hh, MLA_NOPE + MLA_VDIM)
    wk = wkv[:, :, :MLA_NOPE].reshape(MLA_KV_LORA, hh * MLA_NOPE)
    wvt = wkv[:, :, MLA_NOPE:].reshape(MLA_KV_LORA, hh * MLA_VDIM).T
    bf = lambda a: a.astype(BF16)
    return dict(w_z=bf(w_z), wqt=bf(wqt), wk=bf(wk), wvt=bf(wvt),
                wa=bf(p['branch_w_mla']), wb=bf(p['branch_w_gqa']), wc=bf(p['branch_w_ret']),
                wo=bf(p['mix_w_out']),
                ffn1_in=bf(p['ffn1_w_in']), ffn1_out=bf(p['ffn1_w_out']),
                ffn2_in=bf(p['ffn2_w_in']), ffn2_out=bf(p['ffn2_w_out']))


def _mixer(x, xc, m, mc, p, w, ropes, need_ctx):
    (rl_mla, rl_128), (rc_mla, rc_128) = ropes
    d = x.shape[1]
    zl = inproj(x, p['norm_mix'], m[3:5], w['w_z'])
    zc = inproj(xc, p['norm_mix'], mc[3:5], w['w_z'])
    ret_blk, gqa_blk, mla_blk = (3 * d) // 3072, (3 * d + 3072) // 1536, (3 * d + 3072 + 1536) // 1536
    qa_l, ka_l, va_l = mla_prep(zl, mla_blk, p['mla_q_norm'], p['mla_kv_norm'], w['wqt'], w['wk'], w['wvt'], rl_mla)
    qa_c, ka_c, va_c = mla_prep(zc, mla_blk, p['mla_q_norm'], p['mla_kv_norm'], w['wqt'], w['wk'], w['wvt'], rc_mla)
    oa_l = attention(qa_l, ka_l, va_l, ka_c, va_c)
    qb_l, kb_l, vb_l = gqa_prep(zl, gqa_blk, p['gqa_q_norm'], p['gqa_k_norm'], rl_128)
    qb_c, kb_c, vb_c = gqa_prep(zc, gqa_blk, p['gqa_q_norm'], p['gqa_k_norm'], rc_128)
    ob_l = attention(qb_l, kb_l, vb_l, kb_c, vb_c)
    lg_f, lg_b = p['ret_log_decay'][0], p['ret_log_decay'][1]
    s0 = jnp.zeros((RET_HEADS, RET_DK, RET_DV), F32)
    or_cf, st_f = retention(zc, ret_blk, lg_f, rc_128, s0)
    or_c, st_b = retention(zc, ret_blk, lg_b, rc_128, s0, o_fwd=or_cf)
    or_lf, _ = retention(zl, ret_blk, lg_f, rl_128, st_f)
    or_l, _ = retention(zl, ret_blk, lg_b, rl_128, st_b, o_fwd=or_lf)
    x = merge(x, zl, oa_l, ob_l, or_l, w['wa'], w['wb'], w['wc'], w['wo'], m[5])
    if need_ctx:
        oa_c = attention(qa_c, None, None, ka_c, va_c)
        ob_c = attention(qb_c, None, None, kb_c, vb_c)
        xc = merge(xc, zc, oa_c, ob_c, or_c, w['wa'], w['wb'], w['wc'], w['wo'], mc[5])
    return x, xc


def kernel(x, c, ctx, c_ctx, mod_w, mod_b, norm_ffn1, ffn1_w_in, ffn1_w_out, norm_mix, mix_w_in, mla_q_norm, mla_w_uq, mla_kv_norm, mla_w_ukv, gqa_q_norm, gqa_k_norm, ret_log_decay, branch_w_mla, branch_w_gqa, branch_w_ret, mix_w_out, norm_ffn2, ffn2_w_in, ffn2_w_out, final_norm):
    assert x.shape[0] == 1 and ctx.shape[0] == 1 and c.shape[0] == 1
    depth = mod_w.shape[0]
    d = x.shape[2]
    n_tok, n_ctx = x.shape[1], ctx.shape[1]
    ropes = _rope_tables(n_tok, n_ctx)
    xs, xc = x[0], ctx[0]
    cvecs = jnp.concatenate([c, c_ctx[None]], axis=0)
    for layer in range(depth):
        p = {
            'norm_mix': norm_mix[layer], 'mix_w_in': mix_w_in[layer],
            'mla_q_norm': mla_q_norm[layer], 'mla_w_uq': mla_w_uq[layer],
            'mla_kv_norm': mla_kv_norm[layer], 'mla_w_ukv': mla_w_ukv[layer],
            'gqa_q_norm': gqa_q_norm[layer], 'gqa_k_norm': gqa_k_norm[layer],
            'ret_log_decay': ret_log_decay[layer],
            'branch_w_mla': branch_w_mla[layer], 'branch_w_gqa': branch_w_gqa[layer],
            'branch_w_ret': branch_w_ret[layer], 'mix_w_out': mix_w_out[layer],
            'ffn1_w_in': ffn1_w_in[layer], 'ffn1_w_out': ffn1_w_out[layer],
            'ffn2_w_in': ffn2_w_in[layer], 'ffn2_w_out': ffn2_w_out[layer],
        }
        w = _layer_weights(p, d)
        need_ctx = layer < depth - 1
        mods = modvec(cvecs, mod_w[layer], mod_b[layer]).reshape(2, N_MOD, d)
        m, mc = mods[0], mods[1]
        xs = ffn(xs, norm_ffn1[layer], m[0:3], w['ffn1_in'], w['ffn1_out'])
        xc = ffn(xc, norm_ffn1[layer], mc[0:3], w['ffn1_in'], w['ffn1_out'])
        xs, xc = _mixer(xs, xc, m, mc, p, w, ropes, need_ctx)
        last = layer == depth - 1
        xs = ffn(xs, norm_ffn2[layer], m[6:9], w['ffn2_in'], w['ffn2_out'], fin=final_norm if last else None)
        if need_ctx:
            xc = ffn(xc, norm_ffn2[layer], mc[6:9], w['ffn2_in'], w['ffn2_out'])
    return xs[None]
```

```python
import functools
import math

import numpy as np
import jax
import jax.numpy as jnp
from jax import lax
from jax.experimental import pallas as pl
from jax.experimental.pallas import tpu as pltpu

F32 = jnp.float32
BF16 = jnp.bfloat16

GRID_W = 64
N_MOD = 9
MLA_HEADS = 8
MLA_Q_LORA = 512
MLA_KV_LORA = 512
MLA_NOPE = 128
MLA_ROPE = 64
MLA_VDIM = 128
GQA_HEADS = 8
GQA_KV_HEADS = 2
GQA_HEAD_DIM = 128
RET_HEADS = 4
RET_DK = 128
RET_DV = 256
ROPE_THETA = 10000.0
NORM_EPS = 1e-6
LOG2E = math.log2(math.e)
MLA_QSCALE = (MLA_NOPE + MLA_ROPE) ** -0.5 * LOG2E
GQA_QSCALE = GQA_HEAD_DIM ** -0.5 * LOG2E
MLA_QK = 256

LANES = 128
V7X_VMEM_LIMIT = 56 * 1024 * 1024

Z_GATE0 = 0
Z_COLS = 12288


def _cparams(sem):
    return pltpu.CompilerParams(dimension_semantics=sem, vmem_limit_bytes=V7X_VMEM_LIMIT)


def _rms(x):
    return x * lax.rsqrt(jnp.mean(x * x, axis=-1, keepdims=True) + NORM_EPS)


def _dot(a, b):
    return jnp.dot(a, b, preferred_element_type=F32)


def _dot_nt(a, b):
    return lax.dot_general(a, b, (((1,), (1,)), ((), ())), preferred_element_type=F32)


def _modvec_kernel(cb_ref, w_ref, b_ref, o_ref):
    tn = w_ref.shape[1]
    d = w_ref.shape[0]
    for r in range(2):
        cb = cb_ref[r]
        s = cb * jax.nn.sigmoid(cb)
        for g in range(tn // LANES):
            w = w_ref[:, g * LANES:(g + 1) * LANES]
            part = jnp.sum((w * s).reshape(d // 8, 8, LANES), axis=0)
            o_ref[r:r + 1, g * LANES:(g + 1) * LANES] = (
                jnp.sum(part, axis=0, keepdims=True) + b_ref[:, g * LANES:(g + 1) * LANES])


def modvec(cvecs, w, b):
    d, n = w.shape
    tn = 1024
    cb = jnp.broadcast_to(cvecs[:, :, None], (2, d, LANES))
    return pl.pallas_call(
        _modvec_kernel,
        grid=(n // tn,),
        in_specs=[pl.BlockSpec((2, d, LANES), lambda j: (0, 0, 0)),
                  pl.BlockSpec((d, tn), lambda j: (0, j)),
                  pl.BlockSpec((1, tn), lambda j: (0, j))],
        out_specs=pl.BlockSpec((2, tn), lambda j: (0, j)),
        out_shape=jax.ShapeDtypeStruct((2, n), F32),
        compiler_params=_cparams(("arbitrary",)),
    )(cb, w, b.reshape(1, n))


def _ffn_kernel(x_ref, g_ref, mod_ref, wa_ref, wb_ref, wo_ref, fin_ref, o_ref, h_ref, *, final_norm):
    j = pl.program_id(1)

    @pl.when(j == 0)
    def _():
        y = _rms(x_ref[...]) * g_ref[...]
        h = y * (1.0 + mod_ref[1:2, :]) + mod_ref[0:1, :]
        h_ref[...] = h.astype(BF16)
        o_ref[...] = jnp.zeros_like(o_ref)

    h = h_ref[...]
    a = _dot(h, wa_ref[...])
    b = _dot(h, wb_ref[...])
    act = (a * jax.nn.sigmoid(a) * b).astype(BF16)
    o_ref[...] += _dot(act, wo_ref[...])

    @pl.when(j == pl.num_programs(1) - 1)
    def _():
        xn = x_ref[...] + (0.5 * mod_ref[2:3, :]) * o_ref[...]
        if final_norm:
            xn = _rms(xn) * fin_ref[...]
        o_ref[...] = xn


def ffn(x, gain, mod3, w_in, w_out, fin=None):
    m, d = x.shape
    f = w_out.shape[0]
    tm = min(512, m)
    tf = 512
    nf = f // tf
    final_norm = fin is not None
    if fin is None:
        fin = jnp.ones((d,), F32)
    return pl.pallas_call(
        functools.partial(_ffn_kernel, final_norm=final_norm),
        grid=(m // tm, nf),
        in_specs=[pl.BlockSpec((tm, d), lambda i, j: (i, 0)),
                  pl.BlockSpec((1, d), lambda i, j: (0, 0)),
                  pl.BlockSpec((3, d), lambda i, j: (0, 0)),
                  pl.BlockSpec((d, tf), lambda i, j: (0, j)),
                  pl.BlockSpec((d, tf), lambda i, j: (0, j + nf)),
                  pl.BlockSpec((tf, d), lambda i, j: (j, 0)),
                  pl.BlockSpec((1, d), lambda i, j: (0, 0))],
        out_specs=pl.BlockSpec((tm, d), lambda i, j: (i, 0)),
        out_shape=jax.ShapeDtypeStruct((m, d), F32),
        scratch_shapes=[pltpu.VMEM((tm, d), BF16)],
        compiler_params=_cparams(("parallel", "arbitrary")),
    )(x, gain.reshape(1, d), mod3, w_in, w_in, w_out, fin.reshape(1, d))


def _inproj_kernel(x_ref, g_ref, mod_ref, w_ref, o_ref, h_ref):
    @pl.when(pl.program_id(1) == 0)
    def _():
        y = _rms(x_ref[...]) * g_ref[...]
        h_ref[...] = (y * (1.0 + mod_ref[1:2, :]) + mod_ref[0:1, :]).astype(BF16)

    o_ref[...] = _dot(h_ref[...], w_ref[...]).astype(o_ref.dtype)


def inproj(x, gain, mod2, w):
    m, d = x.shape
    n = w.shape[1]
    tm = min(512, m)
    tn = 1536
    return pl.pallas_call(
        _inproj_kernel,
        grid=(m // tm, n // tn),
        in_specs=[pl.BlockSpec((tm, d), lambda i, j: (i, 0)),
                  pl.BlockSpec((1, d), lambda i, j: (0, 0)),
                  pl.BlockSpec((2, d), lambda i, j: (0, 0)),
                  pl.BlockSpec((d, tn), lambda i, j: (0, j))],
        out_specs=pl.BlockSpec((tm, tn), lambda i, j: (i, j)),
        out_shape=jax.ShapeDtypeStruct((m, n), BF16),
        scratch_shapes=[pltpu.VMEM((tm, d), BF16)],
        compiler_params=_cparams(("parallel", "arbitrary")),
    )(x, gain.reshape(1, d), mod2, w)


def _mla_prep_kernel(z_ref, gq_ref, gkv_ref, wqt_ref, wk_ref, wvt_ref, cost_ref, sint_ref,
                     kcos_ref, ksa_ref, ksb_ref, qt_ref, kc_ref, vt_ref):
    cq = z_ref[:, 0:MLA_Q_LORA].astype(F32)
    ckv = z_ref[:, MLA_Q_LORA:MLA_Q_LORA + MLA_KV_LORA].astype(F32)
    kr = z_ref[:, MLA_Q_LORA + MLA_KV_LORA:MLA_Q_LORA + MLA_KV_LORA + LANES].astype(F32)
    cqn = (_rms(cq) * gq_ref[...]).astype(BF16)
    ckvn = (_rms(ckv) * gkv_ref[...]).astype(BF16)
    kr_r = (kr * kcos_ref[...] + pltpu.roll(kr, 32, 1) * ksa_ref[...]
            + pltpu.roll(kr, 96, 1) * ksb_ref[...]).astype(BF16)
    knope = _dot(ckvn, wk_ref[...])
    vt = _dot_nt(wvt_ref[...], ckvn)
    cos_t = cost_ref[...]
    sin_t = sint_ref[...]
    half = MLA_ROPE // 2
    for h in range(MLA_HEADS):
        qt = _dot_nt(wqt_ref[h], cqn) * MLA_QSCALE
        x1 = qt[MLA_NOPE:MLA_NOPE + half]
        x2 = qt[MLA_NOPE + half:MLA_NOPE + 2 * half]
        qt_ref[h, 0:MLA_NOPE, :] = qt[0:MLA_NOPE].astype(BF16)
        qt_ref[h, MLA_NOPE:MLA_NOPE + half, :] = (x1 * cos_t - x2 * sin_t).astype(BF16)
        qt_ref[h, MLA_NOPE + half:MLA_NOPE + 2 * half, :] = (x1 * sin_t + x2 * cos_t).astype(BF16)
        qt_ref[h, MLA_NOPE + 2 * half:, :] = qt[MLA_NOPE + 2 * half:].astype(BF16)
        kc_ref[h, :, 0:MLA_NOPE] = knope[:, h * MLA_NOPE:(h + 1) * MLA_NOPE].astype(BF16)
        kc_ref[h, :, MLA_NOPE:] = kr_r
        vt_ref[h] = vt[h * MLA_VDIM:(h + 1) * MLA_VDIM].astype(BF16)


def mla_prep(z, zblk, gq, gkv, wqt, wk, wvt, rope):
    m = z.shape[0]
    tm = min(512, m)
    cos_t, sin_t, kcos, ksa, ksb = rope
    hh = MLA_HEADS
    return pl.pallas_call(
        _mla_prep_kernel,
        grid=(m // tm,),
        in_specs=[pl.BlockSpec((tm, 1536), lambda i: (i, zblk)),
                  pl.BlockSpec((1, MLA_Q_LORA), lambda i: (0, 0)),
                  pl.BlockSpec((1, MLA_KV_LORA), lambda i: (0, 0)),
                  pl.BlockSpec((hh, MLA_QK, MLA_Q_LORA), lambda i: (0, 0, 0)),
                  pl.BlockSpec((MLA_KV_LORA, hh * MLA_NOPE), lambda i: (0, 0)),
                  pl.BlockSpec((hh * MLA_VDIM, MLA_KV_LORA), lambda i: (0, 0)),
                  pl.BlockSpec((MLA_ROPE // 2, tm), lambda i: (0, i)),
                  pl.BlockSpec((MLA_ROPE // 2, tm), lambda i: (0, i)),
                  pl.BlockSpec((tm, LANES), lambda i: (i, 0)),
                  pl.BlockSpec((tm, LANES), lambda i: (i, 0)),
                  pl.BlockSpec((tm, LANES), lambda i: (i, 0))],
        out_specs=[pl.BlockSpec((hh, MLA_QK, tm), lambda i: (0, 0, i)),
                   pl.BlockSpec((hh, tm, MLA_QK), lambda i: (0, i, 0)),
                   pl.BlockSpec((hh, MLA_VDIM, tm), lambda i: (0, 0, i))],
        out_shape=[jax.ShapeDtypeStruct((hh, MLA_QK, m), BF16),
                   jax.ShapeDtypeStruct((hh, m, MLA_QK), BF16),
                   jax.ShapeDtypeStruct((hh, MLA_VDIM, m), BF16)],
        compiler_params=_cparams(("parallel",)),
    )(z, gq.reshape(1, -1), gkv.reshape(1, -1), wqt, wk, wvt, cos_t, sin_t, kcos, ksa, ksb)


def _rope128(x, cosf, sins):
    return x * cosf + pltpu.roll(x, 64, 1) * sins


def _gqa_prep_kernel(z_ref, gq_ref, gk_ref, cos_ref, sin_ref, qt_ref, k_ref, vt_ref):
    cosf = cos_ref[...]
    sins = sin_ref[...]
    dh = GQA_HEAD_DIM
    for h in range(GQA_HEADS):
        x = z_ref[:, h * dh:(h + 1) * dh].astype(F32)
        q = _rope128(_rms(x) * gq_ref[...], cosf, sins) * GQA_QSCALE
        qt_ref[h] = q.T.astype(BF16)
    k0 = GQA_HEADS * dh
    v0 = k0 + GQA_KV_HEADS * dh
    for h in range(GQA_KV_HEADS):
        x = z_ref[:, k0 + h * dh:k0 + (h + 1) * dh].astype(F32)
        k_ref[h] = _rope128(_rms(x) * gk_ref[...], cosf, sins).astype(BF16)
        v = z_ref[:, v0 + h * dh:v0 + (h + 1) * dh].astype(F32)
        vt_ref[h] = v.T.astype(BF16)


def gqa_prep(z, zblk, gq, gk, rope):
    m = z.shape[0]
    tm = min(512, m)
    cosf, sins = rope
    dh = GQA_HEAD_DIM
    return pl.pallas_call(
        _gqa_prep_kernel,
        grid=(m // tm,),
        in_specs=[pl.BlockSpec((tm, 1536), lambda i: (i, zblk)),
                  pl.BlockSpec((1, dh), lambda i: (0, 0)),
                  pl.BlockSpec((1, dh), lambda i: (0, 0)),
                  pl.BlockSpec((tm, dh), lambda i: (i, 0)),
                  pl.BlockSpec((tm, dh), lambda i: (i, 0))],
        out_specs=[pl.BlockSpec((GQA_HEADS, dh, tm), lambda i: (0, 0, i)),
                   pl.BlockSpec((GQA_KV_HEADS, tm, dh), lambda i: (0, i, 0)),
                   pl.BlockSpec((GQA_KV_HEADS, dh, tm), lambda i: (0, 0, i))],
        out_shape=[jax.ShapeDtypeStruct((GQA_HEADS, dh, m), BF16),
                   jax.ShapeDtypeStruct((GQA_KV_HEADS, m, dh), BF16),
                   jax.ShapeDtypeStruct((GQA_KV_HEADS, dh, m), BF16)],
        compiler_params=_cparams(("parallel",)),
    )(z, gq.reshape(1, dh), gk.reshape(1, dh), cosf, sins)


ATTN_TQ = 256


def _attn_kernel(*refs, tk, n_lat_chunks, chains):
    if n_lat_chunks:
        qt_ref, kl_ref, vlt_ref, kc_ref, vct_ref, o_ref, m_ref, l_ref, acc_ref = refs
    else:
        qt_ref, kc_ref, vct_ref, o_ref, m_ref, l_ref, acc_ref = refs
    dv = acc_ref.shape[1]
    m_ref[...] = jnp.full_like(m_ref, -1e30)
    l_ref[...] = jnp.zeros_like(l_ref)
    acc_ref[...] = jnp.zeros_like(acc_ref)

    def update(k, vt):
        sts = [_dot(k, qt_ref[g, :, q0:q0 + ATTN_TQ]) for (g, q0) in chains]
        ps, alphas = [], []
        for n, st in enumerate(sts):
            m_old = m_ref[n]
            m_new = jnp.maximum(m_old, jnp.max(st, axis=0, keepdims=True))
            p = jnp.exp2(st - m_new)
            alpha = jnp.exp2(m_old - m_new)
            l_ref[n] = alpha * l_ref[n] + jnp.sum(p, axis=0, keepdims=True)
            m_ref[n] = m_new
            ps.append(p.astype(BF16))
            alphas.append(alpha)
        for n in range(len(chains)):
            acc_ref[n] = alphas[n] * acc_ref[n] + _dot(vt, ps[n])

    update(kc_ref[0], vct_ref[0])
    if n_lat_chunks:
        def body(c, carry):
            off = pl.multiple_of(c * tk, tk)
            update(kl_ref[0, pl.ds(off, tk), :], vlt_ref[0, :, pl.ds(off, tk)])
            return carry
        lax.fori_loop(0, n_lat_chunks, body, 0)

    for n, (g, q0) in enumerate(chains):
        o = acc_ref[n] / l_ref[n]
        o_ref[q0:q0 + ATTN_TQ, g * dv:(g + 1) * dv] = o.T.astype(o_ref.dtype)


def attention(qt, k_lat, vt_lat, k_ctx, vt_ctx):
    hq, dq, lq = qt.shape
    hk, lc, _ = k_ctx.shape
    dv = vt_ctx.shape[1]
    grp = hq // hk
    n_chains = 4
    tq = min(ATTN_TQ * max(n_chains // grp, 1), lq)
    chains = [(g, q0) for g in range(grp) for q0 in range(0, tq, ATTN_TQ)]
    if k_lat is not None:
        ll = k_lat.shape[1]
        tk = min(1024, ll)
        n_chunks = ll // tk
    else:
        tk, n_chunks = 0, 0
    in_specs = [pl.BlockSpec((grp, dq, tq), lambda h, i: (h, 0, i))]
    args = [qt]
    if k_lat is not None:
        in_specs += [pl.BlockSpec((1, ll, dq), lambda h, i: (h, 0, 0)),
                     pl.BlockSpec((1, dv, ll), lambda h, i: (h, 0, 0))]
        args += [k_lat, vt_lat]
    in_specs += [pl.BlockSpec((1, lc, dq), lambda h, i: (h, 0, 0)),
                 pl.BlockSpec((1, dv, lc), lambda h, i: (h, 0, 0))]
    args += [k_ctx, vt_ctx]
    nc = len(chains)
    return pl.pallas_call(
        functools.partial(_attn_kernel, tk=tk, n_lat_chunks=n_chunks, chains=chains),
        grid=(hk, lq // tq),
        in_specs=in_specs,
        out_specs=pl.BlockSpec((tq, grp * dv), lambda h, i: (i, h)),
        out_shape=jax.ShapeDtypeStruct((lq, hq * dv), BF16),
        scratch_shapes=[pltpu.VMEM((nc, 1, ATTN_TQ), F32), pltpu.VMEM((nc, 1, ATTN_TQ), F32),
                        pltpu.VMEM((nc, dv, ATTN_TQ), F32)],
        compiler_params=_cparams(("parallel", "arbitrary")),
    )(*args)


def _ret_kernel(*refs, chunk, reverse, finalize):
    if finalize:
        (lg_ref, z_ref, cos_ref, sin_ref, s0_ref, of_ref,
         o_ref, sfin_ref, s_ref, dec_ref, qd_ref, kd_ref) = refs
    else:
        (lg_ref, z_ref, cos_ref, sin_ref, s0_ref,
         o_ref, sfin_ref, s_ref, dec_ref, qd_ref, kd_ref) = refs
        of_ref = None
    i = pl.program_id(0)
    c = chunk
    dk, dvv, nh = RET_DK, RET_DV, RET_HEADS

    @pl.when(i == 0)
    def _():
        s_ref[...] = s0_ref[...]
        row = lax.broadcasted_iota(jnp.int32, (c, c), 0).astype(F32)
        col = lax.broadcasted_iota(jnp.int32, (c, c), 1).astype(F32)
        rel = (col - row) if reverse else (row - col)
        pos = lax.broadcasted_iota(jnp.int32, (c, LANES), 0).astype(F32)
        if reverse:
            pos = (c - 1.0) - pos
        for h in range(nh):
            lg = lg_ref[h]
            dec_ref[h] = jnp.where(rel >= 0, jnp.exp(lg * jnp.maximum(rel, 0.0)), 0.0)
            qd_ref[h] = jnp.exp(lg * (pos + 1.0))
            kd_ref[h] = jnp.exp(lg * ((c - 1.0) - pos))

    cosf = cos_ref[...]
    sins = sin_ref[...]
    q0, k0, v0, g0 = 0, nh * dk, 2 * nh * dk, 2 * nh * dk + nh * dvv
    for h in range(nh):
        q = _rope128(z_ref[:, q0 + h * dk:q0 + (h + 1) * dk].astype(F32), cosf, sins)
        k = _rope128(z_ref[:, k0 + h * dk:k0 + (h + 1) * dk].astype(F32) * (dk ** -0.5), cosf, sins)
        v = z_ref[:, v0 + h * dvv:v0 + (h + 1) * dvv]
        s = s_ref[h]
        att = _dot_nt(q.astype(BF16), k.astype(BF16)) * dec_ref[h]
        o = _dot(att.astype(BF16), v) + _dot((q * qd_ref[h]).astype(BF16), s.astype(BF16))
        kdt = (k * kd_ref[h]).T.astype(BF16)
        cdec = jnp.exp(jnp.full((1, 1), c, F32) * lg_ref[h])
        s_ref[h] = cdec * s + _dot(kdt, v)
        if finalize:
            y = _rms(o + of_ref[:, h * dvv:(h + 1) * dvv])
            g = z_ref[:, g0 + h * dvv:g0 + (h + 1) * dvv].astype(F32)
            o_ref[:, h * dvv:(h + 1) * dvv] = (y * (g * jax.nn.sigmoid(g))).astype(o_ref.dtype)
        else:
            o_ref[:, h * dvv:(h + 1) * dvv] = o

    @pl.when(i == pl.num_programs(0) - 1)
    def _():
        sfin_ref[...] = s_ref[...]


def retention(z, zblk, lg, rope, s0, o_fwd=None):
    m = z.shape[0]
    c = min(256, m)
    n = m // c
    reverse = o_fwd is not None
    cosf, sins = rope
    nh, dk, dvv = RET_HEADS, RET_DK, RET_DV
    blk = (lambda i: (n - 1 - i, 0)) if reverse else (lambda i: (i, 0))
    zmap = (lambda i: (n - 1 - i, zblk)) if reverse else (lambda i: (i, zblk))
    in_specs = [pl.BlockSpec(memory_space=pltpu.SMEM),
                pl.BlockSpec((c, 3072), zmap),
                pl.BlockSpec((c, LANES), blk),
                pl.BlockSpec((c, LANES), blk),
                pl.BlockSpec((nh, dk, dvv), lambda i: (0, 0, 0))]
    args = [lg, z, cosf, sins, s0]
    if reverse:
        in_specs.append(pl.BlockSpec((c, nh * dvv), blk))
        args.append(o_fwd)
    return pl.pallas_call(
        functools.partial(_ret_kernel, chunk=c, reverse=reverse, finalize=reverse),
        grid=(n,),
        in_specs=in_specs,
        out_specs=[pl.BlockSpec((c, nh * dvv), blk),
                   pl.BlockSpec((nh, dk, dvv), lambda i: (0, 0, 0))],
        out_shape=[jax.ShapeDtypeStruct((m, nh * dvv), BF16 if reverse else F32),
                   jax.ShapeDtypeStruct((nh, dk, dvv), F32)],
        scratch_shapes=[pltpu.VMEM((nh, dk, dvv), F32), pltpu.VMEM((nh, c, c), F32),
                        pltpu.VMEM((nh, c, LANES), F32), pltpu.VMEM((nh, c, LANES), F32)],
        compiler_params=_cparams(("arbitrary",)),
    )(*args)


def _merge_kernel(x_ref, g0_ref, g1_ref, g2_ref, oa_ref, ob_ref, oc_ref, wa_ref, wb_ref, wc_ref,
                  wo_ref, mod_ref, o_ref):
    y = jax.nn.sigmoid(g0_ref[...].astype(F32)) * _dot(oa_ref[...], wa_ref[...])
    y += jax.nn.sigmoid(g1_ref[...].astype(F32)) * _dot(ob_ref[...], wb_ref[...])
    y += jax.nn.sigmoid(g2_ref[...].astype(F32)) * _dot(oc_ref[...], wc_ref[...])
    out = _dot(y.astype(BF16), wo_ref[...])
    o_ref[...] = x_ref[...] + mod_ref[...] * out


def merge(x, z, oa, ob, oc, wa, wb, wc, wo, gate):
    m, d = x.shape
    tm = min(256, m)
    const = lambda i: (0, 0)
    wspec = lambda w: pl.BlockSpec(w.shape, const, pipeline_mode=pl.Buffered(1))
    return pl.pallas_call(
        _merge_kernel,
        grid=(m // tm,),
        in_specs=[pl.BlockSpec((tm, d), lambda i: (i, 0)),
                  pl.BlockSpec((tm, d), lambda i: (i, 0)),
                  pl.BlockSpec((tm, d), lambda i: (i, 1)),
                  pl.BlockSpec((tm, d), lambda i: (i, 2)),
                  pl.BlockSpec((tm, oa.shape[1]), lambda i: (i, 0)),
                  pl.BlockSpec((tm, ob.shape[1]), lambda i: (i, 0)),
                  pl.BlockSpec((tm, oc.shape[1]), lambda i: (i, 0)),
                  wspec(wa), wspec(wb), wspec(wc), wspec(wo),
                  pl.BlockSpec((1, d), const)],
        out_specs=pl.BlockSpec((tm, d), lambda i: (i, 0)),
        out_shape=jax.ShapeDtypeStruct((m, d), F32),
        compiler_params=_cparams(("parallel",)),
    )(x, z, z, z, oa, ob, oc, wa, wb, wc, wo, gate.reshape(1, d))


def _rope_tables(n_tok, n_ctx):
    n_rows = n_tok // GRID_W
    rows = jnp.repeat(jnp.arange(n_rows, dtype=F32), GRID_W)
    cols = jnp.tile(jnp.arange(GRID_W, dtype=F32), n_rows)

    def table(dim):
        d_axis = dim // 2
        inv_freq = ROPE_THETA ** (-jnp.arange(0, d_axis, 2, dtype=F32) / d_axis)
        ang = jnp.concatenate([rows[:, None] * inv_freq, cols[:, None] * inv_freq], axis=-1)
        return jnp.cos(ang), jnp.sin(ang)

    def mla_pack(cos, sin):
        t = cos.shape[0]
        z32 = jnp.zeros((t, 32), F32)
        z64 = jnp.zeros((t, 64), F32)
        return (cos.T, sin.T,
                jnp.concatenate([cos, cos, z64], axis=1),
                jnp.concatenate([z32, sin, z64], axis=1),
                jnp.concatenate([-sin, z32, z64], axis=1))

    def pack128(cos, sin):
        return (jnp.concatenate([cos, cos], axis=1), jnp.concatenate([-sin, sin], axis=1))

    cm, sm = table(MLA_ROPE)
    c128, s128 = table(GQA_HEAD_DIM)
    lat = (mla_pack(cm, sm), pack128(c128, s128))
    one = lambda w: jnp.ones((n_ctx, w), F32)
    zero = lambda w: jnp.zeros((n_ctx, w), F32)
    ctx = (mla_pack(one(MLA_ROPE // 2), zero(MLA_ROPE // 2)), pack128(one(64), zero(64)))
    return lat, ctx


def _layer_weights(p, d):
    w = p['mix_w_in']
    sizes = (MLA_Q_LORA, MLA_KV_LORA, MLA_ROPE, GQA_HEADS * GQA_HEAD_DIM, GQA_KV_HEADS * GQA_HEAD_DIM,
             GQA_KV_HEADS * GQA_HEAD_DIM, RET_HEADS * RET_DK, RET_HEADS * RET_DK, RET_HEADS * RET_DV,
             RET_HEADS * RET_DV, 3 * d)
    offs = np.concatenate([[0], np.cumsum(sizes)])
    part = [w[:, offs[i]:offs[i + 1]] for i in range(len(sizes))]
    used = 3 * d + 3072 + 1536 + MLA_Q_LORA + MLA_KV_LORA + MLA_ROPE
    w_z = jnp.concatenate([part[10], part[6], part[7], part[8], part[9], part[3], part[4], part[5],
                           part[0], part[1], part[2], jnp.zeros((d, Z_COLS - used), w.dtype)], axis=1)
    hh = MLA_HEADS
    wq = p['mla_w_uq'].reshape(MLA_Q_LORA, hh, MLA_NOPE + MLA_ROPE)
    wq = jnp.concatenate([wq, jnp.zeros((MLA_Q_LORA, hh, MLA_QK - MLA_NOPE - MLA_ROPE), wq.dtype)], axis=2)
    wqt = wq.transpose(1, 2, 0)
    wkv = p['mla_w_ukv'].reshape(MLA_KV_LORA, hh, MLA_NOPE + MLA_VDIM)
    wk = wkv[:, :, :MLA_NOPE].reshape(MLA_KV_LORA, hh * MLA_NOPE)
    wvt = wkv[:, :, MLA_NOPE:].reshape(MLA_KV_LORA, hh * MLA_VDIM).T
    bf = lambda a: a.astype(BF16)
    return dict(w_z=bf(w_z), wqt=bf(wqt), wk=bf(wk), wvt=bf(wvt),
                wa=bf(p['branch_w_mla']), wb=bf(p['branch_w_gqa']), wc=bf(p['branch_w_ret']),
                wo=bf(p['mix_w_out']),
                ffn1_in=bf(p['ffn1_w_in']), ffn1_out=bf(p['ffn1_w_out']),
                ffn2_in=bf(p['ffn2_w_in']), ffn2_out=bf(p['ffn2_w_out']))


def _mixer(x, xc, m, mc, p, w, ropes, need_ctx):
    (rl_mla, rl_128), (rc_mla, rc_128) = ropes
    d = x.shape[1]
    zl = inproj(x, p['norm_mix'], m[3:5], w['w_z'])
    zc = inproj(xc, p['norm_mix'], mc[3:5], w['w_z'])
    ret_blk, gqa_blk, mla_blk = (3 * d) // 3072, (3 * d + 3072) // 1536, (3 * d + 3072 + 1536) // 1536
    qa_l, ka_l, va_l = mla_prep(zl, mla_blk, p['mla_q_norm'], p['mla_kv_norm'], w['wqt'], w['wk'], w['wvt'], rl_mla)
    qa_c, ka_c, va_c = mla_prep(zc, mla_blk, p['mla_q_norm'], p['mla_kv_norm'], w['wqt'], w['wk'], w['wvt'], rc_mla)
    oa_l = attention(qa_l, ka_l, va_l, ka_c, va_c)
    qb_l, kb_l, vb_l = gqa_prep(zl, gqa_blk, p['gqa_q_norm'], p['gqa_k_norm'], rl_128)
    qb_c, kb_c, vb_c = gqa_prep(zc, gqa_blk, p['gqa_q_norm'], p['gqa_k_norm'], rc_128)
    ob_l = attention(qb_l, kb_l, vb_l, kb_c, vb_c)
    lg_f, lg_b = p['ret_log_decay'][0], p['ret_log_decay'][1]
    s0 = jnp.zeros((RET_HEADS, RET_DK, RET_DV), F32)
    or_cf, st_f = retention(zc, ret_blk, lg_f, rc_128, s0)
    or_c, st_b = retention(zc, ret_blk, lg_b, rc_128, s0, o_fwd=or_cf)
    or_lf, _ = retention(zl, ret_blk, lg_f, rl_128, st_f)
    or_l, _ = retention(zl, ret_blk, lg_b, rl_128, st_b, o_fwd=or_lf)
    x = merge(x, zl, oa_l, ob_l, or_l, w['wa'], w['wb'], w['wc'], w['wo'], m[5])
    if need_ctx:
        oa_c = attention(qa_c, None, None, ka_c, va_c)
        ob_c = attention(qb_c, None, None, kb_c, vb_c)
        xc = merge(xc, zc, oa_c, ob_c, or_c, w['wa'], w['wb'], w['wc'], w['wo'], mc[5])
    return x, xc


def kernel(x, c, ctx, c_ctx, mod_w, mod_b, norm_ffn1, ffn1_w_in, ffn1_w_out, norm_mix, mix_w_in, mla_q_norm, mla_w_uq, mla_kv_norm, mla_w_ukv, gqa_q_norm, gqa_k_norm, ret_log_decay, branch_w_mla, branch_w_gqa, branch_w_ret, mix_w_out, norm_ffn2, ffn2_w_in, ffn2_w_out, final_norm):
    assert x.shape[0] == 1 and ctx.shape[0] == 1 and c.shape[0] == 1
    depth = mod_w.shape[0]
    d = x.shape[2]
    n_tok, n_ctx = x.shape[1], ctx.shape[1]
    ropes = _rope_tables(n_tok, n_ctx)
    xs, xc = x[0], ctx[0]
    cvecs = jnp.concatenate([c, c_ctx[None]], axis=0)
    for layer in range(depth):
        p = {
            'norm_mix': norm_mix[layer], 'mix_w_in': mix_w_in[layer],
            'mla_q_norm': mla_q_norm[layer], 'mla_w_uq': mla_w_uq[layer],
            'mla_kv_norm': mla_kv_norm[layer], 'mla_w_ukv': mla_w_ukv[layer],
            'gqa_q_norm': gqa_q_norm[layer], 'gqa_k_norm': gqa_k_norm[layer],
            'ret_log_decay': ret_log_decay[layer],
            'branch_w_mla': branch_w_mla[layer], 'branch_w_gqa': branch_w_gqa[layer],
            'branch_w_ret': branch_w_ret[layer], 'mix_w_out': mix_w_out[layer],
            'ffn1_w_in': ffn1_w_in[layer], 'ffn1_w_out': ffn1_w_out[layer],
            'ffn2_w_in': ffn2_w_in[layer], 'ffn2_w_out': ffn2_w_out[layer],
        }
        w = _layer_weights(p, d)
        need_ctx = layer < depth - 1
        mods = modvec(cvecs, mod_w[layer], mod_b[layer]).reshape(2, N_MOD, d)
        m, mc = mods[0], mods[1]
        xs = ffn(xs, norm_ffn1[layer], m[0:3], w['ffn1_in'], w['ffn1_out'])
        xc = ffn(xc, norm_ffn1[layer], mc[0:3], w['ffn1_in'], w['ffn1_out'])
        xs, xc = _mixer(xs, xc, m, mc, p, w, ropes, need_ctx)
        last = layer == depth - 1
        xs = ffn(xs, norm_ffn2[layer], m[6:9], w['ffn2_in'], w['ffn2_out'], fin=final_norm if last else None)
        if need_ctx:
            xc = ffn(xc, norm_ffn2[layer], mc[6:9], w['ffn2_in'], w['ffn2_out'])
    return xs[None]
```

```python
import functools
import math

import numpy as np
import jax
import jax.numpy as jnp
from jax import lax
from jax.experimental import pallas as pl
from jax.experimental.pallas import tpu as pltpu

F32 = jnp.float32
BF16 = jnp.bfloat16

GRID_W = 64
N_MOD = 9
MLA_HEADS = 8
MLA_Q_LORA = 512
MLA_KV_LORA = 512
MLA_NOPE = 128
MLA_ROPE = 64
MLA_VDIM = 128
GQA_HEADS = 8
GQA_KV_HEADS = 2
GQA_HEAD_DIM = 128
RET_HEADS = 4
RET_DK = 128
RET_DV = 256
ROPE_THETA = 10000.0
NORM_EPS = 1e-6
LOG2E = math.log2(math.e)
MLA_QSCALE = (MLA_NOPE + MLA_ROPE) ** -0.5 * LOG2E
GQA_QSCALE = GQA_HEAD_DIM ** -0.5 * LOG2E
MLA_QK = 256

LANES = 128
V7X_VMEM_LIMIT = 56 * 1024 * 1024

Z_GATE0 = 0
Z_COLS = 12288


def _cparams(sem):
    return pltpu.CompilerParams(dimension_semantics=sem, vmem_limit_bytes=V7X_VMEM_LIMIT)


def _rms(x):
    return x * lax.rsqrt(jnp.mean(x * x, axis=-1, keepdims=True) + NORM_EPS)


def _dot(a, b):
    return jnp.dot(a, b, preferred_element_type=F32)


def _dot_nt(a, b):
    return lax.dot_general(a, b, (((1,), (1,)), ((), ())), preferred_element_type=F32)


def _modvec_kernel(cb_ref, w_ref, b_ref, o_ref):
    tn = w_ref.shape[1]
    d = w_ref.shape[0]
    for r in range(2):
        cb = cb_ref[r]
        s = cb * jax.nn.sigmoid(cb)
        for g in range(tn // LANES):
            w = w_ref[:, g * LANES:(g + 1) * LANES]
            part = jnp.sum((w * s).reshape(d // 8, 8, LANES), axis=0)
            o_ref[r:r + 1, g * LANES:(g + 1) * LANES] = (
                jnp.sum(part, axis=0, keepdims=True) + b_ref[:, g * LANES:(g + 1) * LANES])


def modvec(cvecs, w, b):
    d, n = w.shape
    tn = 1024
    cb = jnp.broadcast_to(cvecs[:, :, None], (2, d, LANES))
    return pl.pallas_call(
        _modvec_kernel,
        grid=(n // tn,),
        in_specs=[pl.BlockSpec((2, d, LANES), lambda j: (0, 0, 0)),
                  pl.BlockSpec((d, tn), lambda j: (0, j)),
                  pl.BlockSpec((1, tn), lambda j: (0, j))],
        out_specs=pl.BlockSpec((2, tn), lambda j: (0, j)),
        out_shape=jax.ShapeDtypeStruct((2, n), F32),
        compiler_params=_cparams(("arbitrary",)),
    )(cb, w, b.reshape(1, n))


def _ffn_kernel(x_ref, g_ref, mod_ref, wa_ref, wb_ref, wo_ref, fin_ref, o_ref, h_ref, *, final_norm):
    j = pl.program_id(1)

    @pl.when(j == 0)
    def _():
        y = _rms(x_ref[...]) * g_ref[...]
        h = y * (1.0 + mod_ref[1:2, :]) + mod_ref[0:1, :]
        h_ref[...] = h.astype(BF16)
        o_ref[...] = jnp.zeros_like(o_ref)

    h = h_ref[...]
    a = _dot(h, wa_ref[...])
    b = _dot(h, wb_ref[...])
    act = (a * jax.nn.sigmoid(a) * b).astype(BF16)
    o_ref[...] += _dot(act, wo_ref[...])

    @pl.when(j == pl.num_programs(1) - 1)
    def _():
        xn = x_ref[...] + (0.5 * mod_ref[2:3, :]) * o_ref[...]
        if final_norm:
            xn = _rms(xn) * fin_ref[...]
        o_ref[...] = xn


def ffn(x, gain, mod3, w_in, w_out, fin=None):
    m, d = x.shape
    f = w_out.shape[0]
    tm = min(512, m)
    tf = 512
    nf = f // tf
    final_norm = fin is not None
    if fin is None:
        fin = jnp.ones((d,), F32)
    return pl.pallas_call(
        functools.partial(_ffn_kernel, final_norm=final_norm),
        grid=(m // tm, nf),
        in_specs=[pl.BlockSpec((tm, d), lambda i, j: (i, 0)),
                  pl.BlockSpec((1, d), lambda i, j: (0, 0)),
                  pl.BlockSpec((3, d), lambda i, j: (0, 0)),
                  pl.BlockSpec((d, tf), lambda i, j: (0, j)),
                  pl.BlockSpec((d, tf), lambda i, j: (0, j + nf)),
                  pl.BlockSpec((tf, d), lambda i, j: (j, 0)),
                  pl.BlockSpec((1, d), lambda i, j: (0, 0))],
        out_specs=pl.BlockSpec((tm, d), lambda i, j: (i, 0)),
        out_shape=jax.ShapeDtypeStruct((m, d), F32),
        scratch_shapes=[pltpu.VMEM((tm, d), BF16)],
        compiler_params=_cparams(("parallel", "arbitrary")),
    )(x, gain.reshape(1, d), mod3, w_in, w_in, w_out, fin.reshape(1, d))


def _inproj_kernel(x_ref, g_ref, mod_ref, w_ref, o_ref, h_ref):
    @pl.when(pl.program_id(1) == 0)
    def _():
        y = _rms(x_ref[...]) * g_ref[...]
        h_ref[...] = (y * (1.0 + mod_ref[1:2, :]) + mod_ref[0:1, :]).astype(BF16)

    o_ref[...] = _dot(h_ref[...], w_ref[...]).astype(o_ref.dtype)


def inproj(x, gain, mod2, w):
    m, d = x.shape
    n = w.shape[1]
    tm = min(512, m)
    tn = 1536
    return pl.pallas_call(
        _inproj_kernel,
        grid=(m // tm, n // tn),
        in_specs=[pl.BlockSpec((tm, d), lambda i, j: (i, 0)),
                  pl.BlockSpec((1, d), lambda i, j: (0, 0)),
                  pl.BlockSpec((2, d), lambda i, j: (0, 0)),
                  pl.BlockSpec((d, tn), lambda i, j: (0, j))],
        out_specs=pl.BlockSpec((tm, tn), lambda i, j: (i, j)),
        out_shape=jax.ShapeDtypeStruct((m, n), BF16),
        scratch_shapes=[pltpu.VMEM((tm, d), BF16)],
        compiler_params=_cparams(("parallel", "arbitrary")),
    )(x, gain.reshape(1, d), mod2, w)


def _mla_prep_kernel(z_ref, gq_ref, gkv_ref, wqt_ref, wk_ref, wvt_ref, cost_ref, sint_ref,
                     kcos_ref, ksa_ref, ksb_ref, qt_ref, kc_ref, vt_ref):
    cq = z_ref[:, 0:MLA_Q_LORA].astype(F32)
    ckv = z_ref[:, MLA_Q_LORA:MLA_Q_LORA + MLA_KV_LORA].astype(F32)
    kr = z_ref[:, MLA_Q_LORA + MLA_KV_LORA:MLA_Q_LORA + MLA_KV_LORA + LANES].astype(F32)
    cqn = (_rms(cq) * gq_ref[...]).astype(BF16)
    ckvn = (_rms(ckv) * gkv_ref[...]).astype(BF16)
    kr_r = (kr * kcos_ref[...] + pltpu.roll(kr, 32, 1) * ksa_ref[...]
            + pltpu.roll(kr, 96, 1) * ksb_ref[...]).astype(BF16)
    knope = _dot(ckvn, wk_ref[...])
    vt = _dot_nt(wvt_ref[...], ckvn)
    cos_t = cost_ref[...]
    sin_t = sint_ref[...]
    half = MLA_ROPE // 2
    for h in range(MLA_HEADS):
        qt = _dot_nt(wqt_ref[h], cqn) * MLA_QSCALE
        x1 = qt[MLA_NOPE:MLA_NOPE + half]
        x2 = qt[MLA_NOPE + half:MLA_NOPE + 2 * half]
        qt_ref[h, 0:MLA_NOPE, :] = qt[0:MLA_NOPE].astype(BF16)
        qt_ref[h, MLA_NOPE:MLA_NOPE + half, :] = (x1 * cos_t - x2 * sin_t).astype(BF16)
        qt_ref[h, MLA_NOPE + half:MLA_NOPE + 2 * half, :] = (x1 * sin_t + x2 * cos_t).astype(BF16)
        qt_ref[h, MLA_NOPE + 2 * half:, :] = qt[MLA_NOPE + 2 * half:].astype(BF16)
        kc_ref[h, :, 0:MLA_NOPE] = knope[:, h * MLA_NOPE:(h + 1) * MLA_NOPE].astype(BF16)
        kc_ref[h, :, MLA_NOPE:] = kr_r
        vt_ref[h] = vt[h * MLA_VDIM:(h + 1) * MLA_VDIM].astype(BF16)


def mla_prep(z, zblk, gq, gkv, wqt, wk, wvt, rope):
    m = z.shape[0]
    tm = min(512, m)
    cos_t, sin_t, kcos, ksa, ksb = rope
    hh = MLA_HEADS
    return pl.pallas_call(
        _mla_prep_kernel,
        grid=(m // tm,),
        in_specs=[pl.BlockSpec((tm, 1536), lambda i: (i, zblk)),
                  pl.BlockSpec((1, MLA_Q_LORA), lambda i: (0, 0)),
                  pl.BlockSpec((1, MLA_KV_LORA), lambda i: (0, 0)),
                  pl.BlockSpec((hh, MLA_QK, MLA_Q_LORA), lambda i: (0, 0, 0)),
                  pl.BlockSpec((MLA_KV_LORA, hh * MLA_NOPE), lambda i: (0, 0)),
                  pl.BlockSpec((hh * MLA_VDIM, MLA_KV_LORA), lambda i: (0, 0)),
                  pl.BlockSpec((MLA_ROPE // 2, tm), lambda i: (0, i)),
                  pl.BlockSpec((MLA_ROPE // 2, tm), lambda i: (0, i)),
                  pl.BlockSpec((tm, LANES), lambda i: (i, 0)),
                  pl.BlockSpec((tm, LANES), lambda i: (i, 0)),
                  pl.BlockSpec((tm, LANES), lambda i: (i, 0))],
        out_specs=[pl.BlockSpec((hh, MLA_QK, tm), lambda i: (0, 0, i)),
                   pl.BlockSpec((hh, tm, MLA_QK), lambda i: (0, i, 0)),
                   pl.BlockSpec((hh, MLA_VDIM, tm), lambda i: (0, 0, i))],
        out_shape=[jax.ShapeDtypeStruct((hh, MLA_QK, m), BF16),
                   jax.ShapeDtypeStruct((hh, m, MLA_QK), BF16),
                   jax.ShapeDtypeStruct((hh, MLA_VDIM, m), BF16)],
        compiler_params=_cparams(("parallel",)),
    )(z, gq.reshape(1, -1), gkv.reshape(1, -1), wqt, wk, wvt, cos_t, sin_t, kcos, ksa, ksb)


def _rope128(x, cosf, sins):
    return x * cosf + pltpu.roll(x, 64, 1) * sins


def _gqa_prep_kernel(z_ref, gq_ref, gk_ref, cos_ref, sin_ref, qt_ref, k_ref, vt_ref):
    cosf = cos_ref[...]
    sins = sin_ref[...]
    dh = GQA_HEAD_DIM
    for h in range(GQA_HEADS):
        x = z_ref[:, h * dh:(h + 1) * dh].astype(F32)
        q = _rope128(_rms(x) * gq_ref[...], cosf, sins) * GQA_QSCALE
        qt_ref[h] = q.T.astype(BF16)
    k0 = GQA_HEADS * dh
    v0 = k0 + GQA_KV_HEADS * dh
    for h in range(GQA_KV_HEADS):
        x = z_ref[:, k0 + h * dh:k0 + (h + 1) * dh].astype(F32)
        k_ref[h] = _rope128(_rms(x) * gk_ref[...], cosf, sins).astype(BF16)
        v = z_ref[:, v0 + h * dh:v0 + (h + 1) * dh].astype(F32)
        vt_ref[h] = v.T.astype(BF16)


def gqa_prep(z, zblk, gq, gk, rope):
    m = z.shape[0]
    tm = min(512, m)
    cosf, sins = rope
    dh = GQA_HEAD_DIM
    return pl.pallas_call(
        _gqa_prep_kernel,
        grid=(m // tm,),
        in_specs=[pl.BlockSpec((tm, 1536), lambda i: (i, zblk)),
                  pl.BlockSpec((1, dh), lambda i: (0, 0)),
                  pl.BlockSpec((1, dh), lambda i: (0, 0)),
                  pl.BlockSpec((tm, dh), lambda i: (i, 0)),
                  pl.BlockSpec((tm, dh), lambda i: (i, 0))],
        out_specs=[pl.BlockSpec((GQA_HEADS, dh, tm), lambda i: (0, 0, i)),
                   pl.BlockSpec((GQA_KV_HEADS, tm, dh), lambda i: (0, i, 0)),
                   pl.BlockSpec((GQA_KV_HEADS, dh, tm), lambda i: (0, 0, i))],
        out_shape=[jax.ShapeDtypeStruct((GQA_HEADS, dh, m), BF16),
                   jax.ShapeDtypeStruct((GQA_KV_HEADS, m, dh), BF16),
                   jax.ShapeDtypeStruct((GQA_KV_HEADS, dh, m), BF16)],
        compiler_params=_cparams(("parallel",)),
    )(z, gq.reshape(1, dh), gk.reshape(1, dh), cosf, sins)


ATTN_TQ = 256


def _attn_kernel(*refs, tk, n_lat_chunks, chains):
    if n_lat_chunks:
        qt_ref, kl_ref, vlt_ref, kc_ref, vct_ref, o_ref, m_ref, l_ref, acc_ref, s_ref = refs
    else:
        qt_ref, kc_ref, vct_ref, o_ref, m_ref, l_ref, acc_ref, s_ref = refs
    dv = acc_ref.shape[1]
    m_ref[...] = jnp.full_like(m_ref, -1e30)
    l_ref[...] = jnp.zeros_like(l_ref)
    acc_ref[...] = jnp.zeros_like(acc_ref)

    def scores(slot, k):
        rows = k.shape[0]
        for n, (g, q0) in enumerate(chains):
            s_ref[slot, n, 0:rows, :] = _dot(k, qt_ref[g, :, q0:q0 + ATTN_TQ])

    def finish(slot, vt):
        rows = vt.shape[1]
        ps, alphas = [], []
        for n in range(len(chains)):
            m_old = m_ref[n]
            m_new = jnp.maximum(m_old, jnp.max(s_ref[slot, n, 0:rows, :], axis=0, keepdims=True))
            p = jnp.exp2(s_ref[slot, n, 0:rows, :] - m_new)
            alpha = jnp.exp2(m_old - m_new)
            l_ref[n] = alpha * l_ref[n] + jnp.sum(p, axis=0, keepdims=True)
            m_ref[n] = m_new
            ps.append(p.astype(BF16))
            alphas.append(alpha)
        for n in range(len(chains)):
            acc_ref[n] = alphas[n] * acc_ref[n] + _dot(vt, ps[n])

    def lat_k(c):
        return kl_ref[0, pl.ds(pl.multiple_of(c * tk, tk), tk), :]

    def lat_vt(c):
        return vlt_ref[0, :, pl.ds(pl.multiple_of(c * tk, tk), tk)]

    scores(1, kc_ref[0])
    if n_lat_chunks:
        scores(0, lat_k(0))
        finish(1, vct_ref[0])

        def stage(slot_s, k, slot_f, vt):
            def qk(n):
                g, q0 = chains[n]
                s_ref[slot_s, n] = _dot(k, qt_ref[g, :, q0:q0 + ATTN_TQ])

            qk(0)
            for n in range(len(chains)):
                if n + 1 < len(chains):
                    qk(n + 1)
                m_old = m_ref[n]
                m_new = jnp.maximum(m_old, jnp.max(s_ref[slot_f, n], axis=0, keepdims=True))
                p = jnp.exp2(s_ref[slot_f, n] - m_new)
                alpha = jnp.exp2(m_old - m_new)
                l_ref[n] = alpha * l_ref[n] + jnp.sum(p, axis=0, keepdims=True)
                m_ref[n] = m_new
                acc_ref[n] = alpha * acc_ref[n] + _dot(vt, p.astype(BF16))

        def body(i, carry):
            c = 2 * i
            stage(1, lat_k(c + 1), 0, lat_vt(c))
            stage(0, lat_k(c + 2), 1, lat_vt(c + 1))
            return carry
        lax.fori_loop(0, n_lat_chunks // 2 - 1, body, 0)
        c = n_lat_chunks - 2
        scores(1, lat_k(c + 1))
        finish(0, lat_vt(c))
        finish(1, lat_vt(c + 1))
    else:
        finish(1, vct_ref[0])

    for n, (g, q0) in enumerate(chains):
        o = acc_ref[n] / l_ref[n]
        o_ref[q0:q0 + ATTN_TQ, g * dv:(g + 1) * dv] = o.T.astype(o_ref.dtype)


def attention(qt, k_lat, vt_lat, k_ctx, vt_ctx):
    hq, dq, lq = qt.shape
    hk, lc, _ = k_ctx.shape
    dv = vt_ctx.shape[1]
    grp = hq // hk
    n_chains = 4
    tq = min(ATTN_TQ * max(n_chains // grp, 1), lq)
    chains = [(g, q0) for g in range(grp) for q0 in range(0, tq, ATTN_TQ)]
    if k_lat is not None:
        ll = k_lat.shape[1]
        tk = min(1024, ll // 2)
        n_chunks = ll // tk
        assert n_chunks % 2 == 0 and tk >= lc
    else:
        tk, n_chunks = lc, 0
    in_specs = [pl.BlockSpec((grp, dq, tq), lambda h, i: (h, 0, i))]
    args = [qt]
    if k_lat is not None:
        in_specs += [pl.BlockSpec((1, ll, dq), lambda h, i: (h, 0, 0)),
                     pl.BlockSpec((1, dv, ll), lambda h, i: (h, 0, 0))]
        args += [k_lat, vt_lat]
    in_specs += [pl.BlockSpec((1, lc, dq), lambda h, i: (h, 0, 0)),
                 pl.BlockSpec((1, dv, lc), lambda h, i: (h, 0, 0))]
    args += [k_ctx, vt_ctx]
    nc = len(chains)
    return pl.pallas_call(
        functools.partial(_attn_kernel, tk=tk, n_lat_chunks=n_chunks, chains=chains),
        grid=(hk, lq // tq),
        in_specs=in_specs,
        out_specs=pl.BlockSpec((tq, grp * dv), lambda h, i: (i, h)),
        out_shape=jax.ShapeDtypeStruct((lq, hq * dv), BF16),
        scratch_shapes=[pltpu.VMEM((nc, 1, ATTN_TQ), F32), pltpu.VMEM((nc, 1, ATTN_TQ), F32),
                        pltpu.VMEM((nc, dv, ATTN_TQ), F32), pltpu.VMEM((2, nc, tk, ATTN_TQ), F32)],
        compiler_params=_cparams(("parallel", "arbitrary")),
    )(*args)


def _ret_kernel(*refs, chunk, reverse, finalize):
    if finalize:
        (lg_ref, z_ref, cos_ref, sin_ref, s0_ref, of_ref,
         o_ref, sfin_ref, s_ref, dec_ref, qd_ref, kd_ref) = refs
    else:
        (lg_ref, z_ref, cos_ref, sin_ref, s0_ref,
         o_ref, sfin_ref, s_ref, dec_ref, qd_ref, kd_ref) = refs
        of_ref = None
    i = pl.program_id(0)
    c = chunk
    dk, dvv, nh = RET_DK, RET_DV, RET_HEADS

    @pl.when(i == 0)
    def _():
        s_ref[...] = s0_ref[...]
        row = lax.broadcasted_iota(jnp.int32, (c, c), 0).astype(F32)
        col = lax.broadcasted_iota(jnp.int32, (c, c), 1).astype(F32)
        rel = (col - row) if reverse else (row - col)
        pos = lax.broadcasted_iota(jnp.int32, (c, LANES), 0).astype(F32)
        if reverse:
            pos = (c - 1.0) - pos
        for h in range(nh):
            lg = lg_ref[h]
            dec_ref[h] = jnp.where(rel >= 0, jnp.exp(lg * jnp.maximum(rel, 0.0)), 0.0)
            qd_ref[h] = jnp.exp(lg * (pos + 1.0))
            kd_ref[h] = jnp.exp(lg * ((c - 1.0) - pos))

    cosf = cos_ref[...]
    sins = sin_ref[...]
    q0, k0, v0, g0 = 0, nh * dk, 2 * nh * dk, 2 * nh * dk + nh * dvv
    for h in range(nh):
        q = _rope128(z_ref[:, q0 + h * dk:q0 + (h + 1) * dk].astype(F32), cosf, sins)
        k = _rope128(z_ref[:, k0 + h * dk:k0 + (h + 1) * dk].astype(F32) * (dk ** -0.5), cosf, sins)
        v = z_ref[:, v0 + h * dvv:v0 + (h + 1) * dvv]
        s = s_ref[h]
        att = _dot_nt(q.astype(BF16), k.astype(BF16)) * dec_ref[h]
        o = _dot(att.astype(BF16), v) + _dot((q * qd_ref[h]).astype(BF16), s.astype(BF16))
        kdt = (k * kd_ref[h]).T.astype(BF16)
        cdec = jnp.exp(jnp.full((1, 1), c, F32) * lg_ref[h])
        s_ref[h] = cdec * s + _dot(kdt, v)
        if finalize:
            y = _rms(o + of_ref[:, h * dvv:(h + 1) * dvv])
            g = z_ref[:, g0 + h * dvv:g0 + (h + 1) * dvv].astype(F32)
            o_ref[:, h * dvv:(h + 1) * dvv] = (y * (g * jax.nn.sigmoid(g))).astype(o_ref.dtype)
        else:
            o_ref[:, h * dvv:(h + 1) * dvv] = o

    @pl.when(i == pl.num_programs(0) - 1)
    def _():
        sfin_ref[...] = s_ref[...]


def retention(z, zblk, lg, rope, s0, o_fwd=None):
    m = z.shape[0]
    c = min(256, m)
    n = m // c
    reverse = o_fwd is not None
    cosf, sins = rope
    nh, dk, dvv = RET_HEADS, RET_DK, RET_DV
    blk = (lambda i: (n - 1 - i, 0)) if reverse else (lambda i: (i, 0))
    zmap = (lambda i: (n - 1 - i, zblk)) if reverse else (lambda i: (i, zblk))
    in_specs = [pl.BlockSpec(memory_space=pltpu.SMEM),
                pl.BlockSpec((c, 3072), zmap),
                pl.BlockSpec((c, LANES), blk),
                pl.BlockSpec((c, LANES), blk),
                pl.BlockSpec((nh, dk, dvv), lambda i: (0, 0, 0))]
    args = [lg, z, cosf, sins, s0]
    if reverse:
        in_specs.append(pl.BlockSpec((c, nh * dvv), blk))
        args.append(o_fwd)
    return pl.pallas_call(
        functools.partial(_ret_kernel, chunk=c, reverse=reverse, finalize=reverse),
        grid=(n,),
        in_specs=in_specs,
        out_specs=[pl.BlockSpec((c, nh * dvv), blk),
                   pl.BlockSpec((nh, dk, dvv), lambda i: (0, 0, 0))],
        out_shape=[jax.ShapeDtypeStruct((m, nh * dvv), BF16 if reverse else F32),
                   jax.ShapeDtypeStruct((nh, dk, dvv), F32)],
        scratch_shapes=[pltpu.VMEM((nh, dk, dvv), F32), pltpu.VMEM((nh, c, c), F32),
                        pltpu.VMEM((nh, c, LANES), F32), pltpu.VMEM((nh, c, LANES), F32)],
        compiler_params=_cparams(("arbitrary",)),
    )(*args)


def _merge_kernel(x_ref, g0_ref, g1_ref, g2_ref, oa_ref, ob_ref, oc_ref, wa_ref, wb_ref, wc_ref,
                  wo_ref, mod_ref, o_ref):
    y = jax.nn.sigmoid(g0_ref[...].astype(F32)) * _dot(oa_ref[...], wa_ref[...])
    y += jax.nn.sigmoid(g1_ref[...].astype(F32)) * _dot(ob_ref[...], wb_ref[...])
    y += jax.nn.sigmoid(g2_ref[...].astype(F32)) * _dot(oc_ref[...], wc_ref[...])
    out = _dot(y.astype(BF16), wo_ref[...])
    o_ref[...] = x_ref[...] + mod_ref[...] * out


def merge(x, z, oa, ob, oc, wa, wb, wc, wo, gate):
    m, d = x.shape
    tm = min(256, m)
    const = lambda i: (0, 0)
    wspec = lambda w: pl.BlockSpec(w.shape, const, pipeline_mode=pl.Buffered(1))
    return pl.pallas_call(
        _merge_kernel,
        grid=(m // tm,),
        in_specs=[pl.BlockSpec((tm, d), lambda i: (i, 0)),
                  pl.BlockSpec((tm, d), lambda i: (i, 0)),
                  pl.BlockSpec((tm, d), lambda i: (i, 1)),
                  pl.BlockSpec((tm, d), lambda i: (i, 2)),
                  pl.BlockSpec((tm, oa.shape[1]), lambda i: (i, 0)),
                  pl.BlockSpec((tm, ob.shape[1]), lambda i: (i, 0)),
                  pl.BlockSpec((tm, oc.shape[1]), lambda i: (i, 0)),
                  wspec(wa), wspec(wb), wspec(wc), wspec(wo),
                  pl.BlockSpec((1, d), const)],
        out_specs=pl.BlockSpec((tm, d), lambda i: (i, 0)),
        out_shape=jax.ShapeDtypeStruct((m, d), F32),
        compiler_params=_cparams(("parallel",)),
    )(x, z, z, z, oa, ob, oc, wa, wb, wc, wo, gate.reshape(1, d))


def _rope_tables(n_tok, n_ctx):
    n_rows = n_tok // GRID_W
    rows = jnp.repeat(jnp.arange(n_rows, dtype=F32), GRID_W)
    cols = jnp.tile(jnp.arange(GRID_W, dtype=F32), n_rows)

    def table(dim):
        d_axis = dim // 2
        inv_freq = ROPE_THETA ** (-jnp.arange(0, d_axis, 2, dtype=F32) / d_axis)
        ang = jnp.concatenate([rows[:, None] * inv_freq, cols[:, None] * inv_freq], axis=-1)
        return jnp.cos(ang), jnp.sin(ang)

    def mla_pack(cos, sin):
        t = cos.shape[0]
        z32 = jnp.zeros((t, 32), F32)
        z64 = jnp.zeros((t, 64), F32)
        return (cos.T, sin.T,
                jnp.concatenate([cos, cos, z64], axis=1),
                jnp.concatenate([z32, sin, z64], axis=1),
                jnp.concatenate([-sin, z32, z64], axis=1))

    def pack128(cos, sin):
        return (jnp.concatenate([cos, cos], axis=1), jnp.concatenate([-sin, sin], axis=1))

    cm, sm = table(MLA_ROPE)
    c128, s128 = table(GQA_HEAD_DIM)
    lat = (mla_pack(cm, sm), pack128(c128, s128))
    one = lambda w: jnp.ones((n_ctx, w), F32)
    zero = lambda w: jnp.zeros((n_ctx, w), F32)
    ctx = (mla_pack(one(MLA_ROPE // 2), zero(MLA_ROPE // 2)), pack128(one(64), zero(64)))
    return lat, ctx


def _layer_weights(p, d):
    w = p['mix_w_in']
    sizes = (MLA_Q_LORA, MLA_KV_LORA, MLA_ROPE, GQA_HEADS * GQA_HEAD_DIM, GQA_KV_HEADS * GQA_HEAD_DIM,
             GQA_KV_HEADS * GQA_HEAD_DIM, RET_HEADS * RET_DK, RET_HEADS * RET_DK, RET_HEADS * RET_DV,
             RET_HEADS * RET_DV, 3 * d)
    offs = np.concatenate([[0], np.cumsum(sizes)])
    part = [w[:, offs[i]:offs[i + 1]] for i in range(len(sizes))]
    used = 3 * d + 3072 + 1536 + MLA_Q_LORA + MLA_KV_LORA + MLA_ROPE
    w_z = jnp.concatenate([part[10], part[6], part[7], part[8], part[9], part[3], part[4], part[5],
                           part[0], part[1], part[2], jnp.zeros((d, Z_COLS - used), w.dtype)], axis=1)
    hh = MLA_HEADS
    wq = p['mla_w_uq'].reshape(MLA_Q_LORA, hh, MLA_NOPE + MLA_ROPE)
    wq = jnp.concatenate([wq, jnp.zeros((MLA_Q_LORA, hh, MLA_QK - MLA_NOPE - MLA_ROPE), wq.dtype)], axis=2)
    wqt = wq.transpose(1, 2, 0)
    wkv = p['mla_w_ukv'].reshape(MLA_KV_LORA, hh, MLA_NOPE + MLA_VDIM)
    wk = wkv[:, :, :MLA_NOPE].reshape(MLA_KV_LORA, hh * MLA_NOPE)
    wvt = wkv[:, :, MLA_NOPE:].reshape(MLA_KV_LORA, hh * MLA_VDIM).T
    bf = lambda a: a.astype(BF16)
    return dict(w_z=bf(w_z), wqt=bf(wqt), wk=bf(wk), wvt=bf(wvt),
                wa=bf(p['branch_w_mla']), wb=bf(p['branch_w_gqa']), wc=bf(p['branch_w_ret']),
                wo=bf(p['mix_w_out']),
                ffn1_in=bf(p['ffn1_w_in']), ffn1_out=bf(p['ffn1_w_out']),
                ffn2_in=bf(p['ffn2_w_in']), ffn2_out=bf(p['ffn2_w_out']))


def _mixer(x, xc, m, mc, p, w, ropes, need_ctx):
    (rl_mla, rl_128), (rc_mla, rc_128) = ropes
    d = x.shape[1]
    zl = inproj(x, p['norm_mix'], m[3:5], w['w_z'])
    zc = inproj(xc, p['norm_mix'], mc[3:5], w['w_z'])
    ret_blk, gqa_blk, mla_blk = (3 * d) // 3072, (3 * d + 3072) // 1536, (3 * d + 3072 + 1536) // 1536
    qa_l, ka_l, va_l = mla_prep(zl, mla_blk, p['mla_q_norm'], p['mla_kv_norm'], w['wqt'], w['wk'], w['wvt'], rl_mla)
    qa_c, ka_c, va_c = mla_prep(zc, mla_blk, p['mla_q_norm'], p['mla_kv_norm'], w['wqt'], w['wk'], w['wvt'], rc_mla)
    oa_l = attention(qa_l, ka_l, va_l, ka_c, va_c)
    qb_l, kb_l, vb_l = gqa_prep(zl, gqa_blk, p['gqa_q_norm'], p['gqa_k_norm'], rl_128)
    qb_c, kb_c, vb_c = gqa_prep(zc, gqa_blk, p['gqa_q_norm'], p['gqa_k_norm'], rc_128)
    ob_l = attention(qb_l, kb_l, vb_l, kb_c, vb_c)
    lg_f, lg_b = p['ret_log_decay'][0], p['ret_log_decay'][1]
    s0 = jnp.zeros((RET_HEADS, RET_DK, RET_DV), F32)
    or_cf, st_f = retention(zc, ret_blk, lg_f, rc_128, s0)
    or_c, st_b = retention(zc, ret_blk, lg_b, rc_128, s0, o_fwd=or_cf)
    or_lf, _ = retention(zl, ret_blk, lg_f, rl_128, st_f)
    or_l, _ = retention(zl, ret_blk, lg_b, rl_128, st_b, o_fwd=or_lf)
    x = merge(x, zl, oa_l, ob_l, or_l, w['wa'], w['wb'], w['wc'], w['wo'], m[5])
    if need_ctx:
        oa_c = attention(qa_c, None, None, ka_c, va_c)
        ob_c = attention(qb_c, None, None, kb_c, vb_c)
        xc = merge(xc, zc, oa_c, ob_c, or_c, w['wa'], w['wb'], w['wc'], w['wo'], mc[5])
    return x, xc


def kernel(x, c, ctx, c_ctx, mod_w, mod_b, norm_ffn1, ffn1_w_in, ffn1_w_out, norm_mix, mix_w_in, mla_q_norm, mla_w_uq, mla_kv_norm, mla_w_ukv, gqa_q_norm, gqa_k_norm, ret_log_decay, branch_w_mla, branch_w_gqa, branch_w_ret, mix_w_out, norm_ffn2, ffn2_w_in, ffn2_w_out, final_norm):
    assert x.shape[0] == 1 and ctx.shape[0] == 1 and c.shape[0] == 1
    depth = mod_w.shape[0]
    d = x.shape[2]
    n_tok, n_ctx = x.shape[1], ctx.shape[1]
    ropes = _rope_tables(n_tok, n_ctx)
    xs, xc = x[0], ctx[0]
    cvecs = jnp.concatenate([c, c_ctx[None]], axis=0)
    for layer in range(depth):
        p = {
            'norm_mix': norm_mix[layer], 'mix_w_in': mix_w_in[layer],
            'mla_q_norm': mla_q_norm[layer], 'mla_w_uq': mla_w_uq[layer],
            'mla_kv_norm': mla_kv_norm[layer], 'mla_w_ukv': mla_w_ukv[layer],
            'gqa_q_norm': gqa_q_norm[layer], 'gqa_k_norm': gqa_k_norm[layer],
            'ret_log_decay': ret_log_decay[layer],
            'branch_w_mla': branch_w_mla[layer], 'branch_w_gqa': branch_w_gqa[layer],
            'branch_w_ret': branch_w_ret[layer], 'mix_w_out': mix_w_out[layer],
            'ffn1_w_in': ffn1_w_in[layer], 'ffn1_w_out': ffn1_w_out[layer],
            'ffn2_w_in': ffn2_w_in[layer], 'ffn2_w_out': ffn2_w_out[layer],
        }
        w = _layer_weights(p, d)
        need_ctx = layer < depth - 1
        mods = modvec(cvecs, mod_w[layer], mod_b[layer]).reshape(2, N_MOD, d)
        m, mc = mods[0], mods[1]
        xs = ffn(xs, norm_ffn1[layer], m[0:3], w['ffn1_in'], w['ffn1_out'])
        xc = ffn(xc, norm_ffn1[layer], mc[0:3], w['ffn1_in'], w['ffn1_out'])
        xs, xc = _mixer(xs, xc, m, mc, p, w, ropes, need_ctx)
        last = layer == depth - 1
        xs = ffn(xs, norm_ffn2[layer], m[6:9], w['ffn2_in'], w['ffn2_out'], fin=final_norm if last else None)
        if need_ctx:
            xc = ffn(xc, norm_ffn2[layer], mc[6:9], w['ffn2_in'], w['ffn2_out'])
    return xs[None]
```

```python
import functools
import math

import numpy as np
import jax
import jax.numpy as jnp
from jax import lax
from jax.experimental import pallas as pl
from jax.experimental.pallas import tpu as pltpu

F32 = jnp.float32
BF16 = jnp.bfloat16

GRID_W = 64
N_MOD = 9
MLA_HEADS = 8
MLA_Q_LORA = 512
MLA_KV_LORA = 512
MLA_NOPE = 128
MLA_ROPE = 64
MLA_VDIM = 128
GQA_HEADS = 8
GQA_KV_HEADS = 2
GQA_HEAD_DIM = 128
RET_HEADS = 4
RET_DK = 128
RET_DV = 256
ROPE_THETA = 10000.0
NORM_EPS = 1e-6
LOG2E = math.log2(math.e)
MLA_QSCALE = (MLA_NOPE + MLA_ROPE) ** -0.5 * LOG2E
GQA_QSCALE = GQA_HEAD_DIM ** -0.5 * LOG2E
MLA_QK = 256

LANES = 128
V7X_VMEM_LIMIT = 56 * 1024 * 1024

Z_GATE0 = 0
Z_COLS = 12288


def _cparams(sem):
    return pltpu.CompilerParams(dimension_semantics=sem, vmem_limit_bytes=V7X_VMEM_LIMIT)


def _rms(x):
    return x * lax.rsqrt(jnp.mean(x * x, axis=-1, keepdims=True) + NORM_EPS)


def _dot(a, b):
    return jnp.dot(a, b, preferred_element_type=F32)


def _dot_nt(a, b):
    return lax.dot_general(a, b, (((1,), (1,)), ((), ())), preferred_element_type=F32)


def _modvec_kernel(cb_ref, w_ref, b_ref, o_ref):
    tn = w_ref.shape[1]
    d = w_ref.shape[0]
    for r in range(2):
        cb = cb_ref[r]
        s = cb * jax.nn.sigmoid(cb)
        for g in range(tn // LANES):
            w = w_ref[:, g * LANES:(g + 1) * LANES]
            part = jnp.sum((w * s).reshape(d // 8, 8, LANES), axis=0)
            o_ref[r:r + 1, g * LANES:(g + 1) * LANES] = (
                jnp.sum(part, axis=0, keepdims=True) + b_ref[:, g * LANES:(g + 1) * LANES])


def modvec(cvecs, w, b):
    d, n = w.shape
    tn = 1024
    cb = jnp.broadcast_to(cvecs[:, :, None], (2, d, LANES))
    return pl.pallas_call(
        _modvec_kernel,
        grid=(n // tn,),
        in_specs=[pl.BlockSpec((2, d, LANES), lambda j: (0, 0, 0)),
                  pl.BlockSpec((d, tn), lambda j: (0, j)),
                  pl.BlockSpec((1, tn), lambda j: (0, j))],
        out_specs=pl.BlockSpec((2, tn), lambda j: (0, j)),
        out_shape=jax.ShapeDtypeStruct((2, n), F32),
        compiler_params=_cparams(("arbitrary",)),
    )(cb, w, b.reshape(1, n))


def _ffn_kernel(x_ref, g_ref, mod_ref, wa_ref, wb_ref, wo_ref, fin_ref, o_ref, h_ref, *, final_norm):
    j = pl.program_id(1)

    @pl.when(j == 0)
    def _():
        y = _rms(x_ref[...]) * g_ref[...]
        h = y * (1.0 + mod_ref[1:2, :]) + mod_ref[0:1, :]
        h_ref[...] = h.astype(BF16)
        o_ref[...] = jnp.zeros_like(o_ref)

    h = h_ref[...]
    a = _dot(h, wa_ref[...])
    b = _dot(h, wb_ref[...])
    act = (a * jax.nn.sigmoid(a) * b).astype(BF16)
    o_ref[...] += _dot(act, wo_ref[...])

    @pl.when(j == pl.num_programs(1) - 1)
    def _():
        xn = x_ref[...] + (0.5 * mod_ref[2:3, :]) * o_ref[...]
        if final_norm:
            xn = _rms(xn) * fin_ref[...]
        o_ref[...] = xn


def ffn(x, gain, mod3, w_in, w_out, fin=None):
    m, d = x.shape
    f = w_out.shape[0]
    tm = min(512, m)
    tf = 512
    nf = f // tf
    final_norm = fin is not None
    if fin is None:
        fin = jnp.ones((d,), F32)
    return pl.pallas_call(
        functools.partial(_ffn_kernel, final_norm=final_norm),
        grid=(m // tm, nf),
        in_specs=[pl.BlockSpec((tm, d), lambda i, j: (i, 0)),
                  pl.BlockSpec((1, d), lambda i, j: (0, 0)),
                  pl.BlockSpec((3, d), lambda i, j: (0, 0)),
                  pl.BlockSpec((d, tf), lambda i, j: (0, j)),
                  pl.BlockSpec((d, tf), lambda i, j: (0, j + nf)),
                  pl.BlockSpec((tf, d), lambda i, j: (j, 0)),
                  pl.BlockSpec((1, d), lambda i, j: (0, 0))],
        out_specs=pl.BlockSpec((tm, d), lambda i, j: (i, 0)),
        out_shape=jax.ShapeDtypeStruct((m, d), F32),
        scratch_shapes=[pltpu.VMEM((tm, d), BF16)],
        compiler_params=_cparams(("parallel", "arbitrary")),
    )(x, gain.reshape(1, d), mod3, w_in, w_in, w_out, fin.reshape(1, d))


def _inproj_kernel(x_ref, g_ref, mod_ref, w_ref, o_ref, h_ref):
    @pl.when(pl.program_id(1) == 0)
    def _():
        y = _rms(x_ref[...]) * g_ref[...]
        h_ref[...] = (y * (1.0 + mod_ref[1:2, :]) + mod_ref[0:1, :]).astype(BF16)

    o_ref[...] = _dot(h_ref[...], w_ref[...]).astype(o_ref.dtype)


def inproj(x, gain, mod2, w):
    m, d = x.shape
    n = w.shape[1]
    tm = min(512, m)
    tn = 1536
    return pl.pallas_call(
        _inproj_kernel,
        grid=(m // tm, n // tn),
        in_specs=[pl.BlockSpec((tm, d), lambda i, j: (i, 0)),
                  pl.BlockSpec((1, d), lambda i, j: (0, 0)),
                  pl.BlockSpec((2, d), lambda i, j: (0, 0)),
                  pl.BlockSpec((d, tn), lambda i, j: (0, j))],
        out_specs=pl.BlockSpec((tm, tn), lambda i, j: (i, j)),
        out_shape=jax.ShapeDtypeStruct((m, n), BF16),
        scratch_shapes=[pltpu.VMEM((tm, d), BF16)],
        compiler_params=_cparams(("parallel", "arbitrary")),
    )(x, gain.reshape(1, d), mod2, w)


def _mla_prep_kernel(z_ref, gq_ref, gkv_ref, wqt_ref, wk_ref, wvt_ref, cost_ref, sint_ref,
                     kcos_ref, ksa_ref, ksb_ref, qt_ref, kc_ref, vt_ref):
    cq = z_ref[:, 0:MLA_Q_LORA].astype(F32)
    ckv = z_ref[:, MLA_Q_LORA:MLA_Q_LORA + MLA_KV_LORA].astype(F32)
    kr = z_ref[:, MLA_Q_LORA + MLA_KV_LORA:MLA_Q_LORA + MLA_KV_LORA + LANES].astype(F32)
    cqn = (_rms(cq) * gq_ref[...]).astype(BF16)
    ckvn = (_rms(ckv) * gkv_ref[...]).astype(BF16)
    kr_r = (kr * kcos_ref[...] + pltpu.roll(kr, 32, 1) * ksa_ref[...]
            + pltpu.roll(kr, 96, 1) * ksb_ref[...]).astype(BF16)
    knope = _dot(ckvn, wk_ref[...])
    vt = _dot_nt(wvt_ref[...], ckvn)
    cos_t = cost_ref[...]
    sin_t = sint_ref[...]
    half = MLA_ROPE // 2
    for h in range(MLA_HEADS):
        qt = _dot_nt(wqt_ref[h], cqn) * MLA_QSCALE
        x1 = qt[MLA_NOPE:MLA_NOPE + half]
        x2 = qt[MLA_NOPE + half:MLA_NOPE + 2 * half]
        qt_ref[h, 0:MLA_NOPE, :] = qt[0:MLA_NOPE].astype(BF16)
        qt_ref[h, MLA_NOPE:MLA_NOPE + half, :] = (x1 * cos_t - x2 * sin_t).astype(BF16)
        qt_ref[h, MLA_NOPE + half:MLA_NOPE + 2 * half, :] = (x1 * sin_t + x2 * cos_t).astype(BF16)
        qt_ref[h, MLA_NOPE + 2 * half:, :] = qt[MLA_NOPE + 2 * half:].astype(BF16)
        kc_ref[h, :, 0:MLA_NOPE] = knope[:, h * MLA_NOPE:(h + 1) * MLA_NOPE].astype(BF16)
        kc_ref[h, :, MLA_NOPE:] = kr_r
        vt_ref[h, 0:MLA_VDIM, :] = vt[h * MLA_VDIM:(h + 1) * MLA_VDIM].astype(BF16)
        vt_ref[h, MLA_VDIM:, :] = _vt_pad_rows(vt.shape[1])


def mla_prep(z, zblk, gq, gkv, wqt, wk, wvt, rope):
    m = z.shape[0]
    tm = min(512, m)
    cos_t, sin_t, kcos, ksa, ksb = rope
    hh = MLA_HEADS
    return pl.pallas_call(
        _mla_prep_kernel,
        grid=(m // tm,),
        in_specs=[pl.BlockSpec((tm, 1536), lambda i: (i, zblk)),
                  pl.BlockSpec((1, MLA_Q_LORA), lambda i: (0, 0)),
                  pl.BlockSpec((1, MLA_KV_LORA), lambda i: (0, 0)),
                  pl.BlockSpec((hh, MLA_QK, MLA_Q_LORA), lambda i: (0, 0, 0)),
                  pl.BlockSpec((MLA_KV_LORA, hh * MLA_NOPE), lambda i: (0, 0)),
                  pl.BlockSpec((hh * MLA_VDIM, MLA_KV_LORA), lambda i: (0, 0)),
                  pl.BlockSpec((MLA_ROPE // 2, tm), lambda i: (0, i)),
                  pl.BlockSpec((MLA_ROPE // 2, tm), lambda i: (0, i)),
                  pl.BlockSpec((tm, LANES), lambda i: (i, 0)),
                  pl.BlockSpec((tm, LANES), lambda i: (i, 0)),
                  pl.BlockSpec((tm, LANES), lambda i: (i, 0))],
        out_specs=[pl.BlockSpec((hh, MLA_QK, tm), lambda i: (0, 0, i)),
                   pl.BlockSpec((hh, tm, MLA_QK), lambda i: (0, i, 0)),
                   pl.BlockSpec((hh, MLA_VDIM + VT_PAD, tm), lambda i: (0, 0, i))],
        out_shape=[jax.ShapeDtypeStruct((hh, MLA_QK, m), BF16),
                   jax.ShapeDtypeStruct((hh, m, MLA_QK), BF16),
                   jax.ShapeDtypeStruct((hh, MLA_VDIM + VT_PAD, m), BF16)],
        compiler_params=_cparams(("parallel",)),
    )(z, gq.reshape(1, -1), gkv.reshape(1, -1), wqt, wk, wvt, cos_t, sin_t, kcos, ksa, ksb)


def _rope128(x, cosf, sins):
    return x * cosf + pltpu.roll(x, 64, 1) * sins


def _gqa_prep_kernel(z_ref, gq_ref, gk_ref, cos_ref, sin_ref, qt_ref, k_ref, vt_ref):
    cosf = cos_ref[...]
    sins = sin_ref[...]
    dh = GQA_HEAD_DIM
    for h in range(GQA_HEADS):
        x = z_ref[:, h * dh:(h + 1) * dh].astype(F32)
        q = _rope128(_rms(x) * gq_ref[...], cosf, sins) * GQA_QSCALE
        qt_ref[h] = q.T.astype(BF16)
    k0 = GQA_HEADS * dh
    v0 = k0 + GQA_KV_HEADS * dh
    for h in range(GQA_KV_HEADS):
        x = z_ref[:, k0 + h * dh:k0 + (h + 1) * dh].astype(F32)
        k_ref[h] = _rope128(_rms(x) * gk_ref[...], cosf, sins).astype(BF16)
        v = z_ref[:, v0 + h * dh:v0 + (h + 1) * dh].astype(F32)
        vt_ref[h, 0:dh, :] = v.T.astype(BF16)
        vt_ref[h, dh:, :] = _vt_pad_rows(v.shape[0])


def gqa_prep(z, zblk, gq, gk, rope):
    m = z.shape[0]
    tm = min(512, m)
    cosf, sins = rope
    dh = GQA_HEAD_DIM
    return pl.pallas_call(
        _gqa_prep_kernel,
        grid=(m // tm,),
        in_specs=[pl.BlockSpec((tm, 1536), lambda i: (i, zblk)),
                  pl.BlockSpec((1, dh), lambda i: (0, 0)),
                  pl.BlockSpec((1, dh), lambda i: (0, 0)),
                  pl.BlockSpec((tm, dh), lambda i: (i, 0)),
                  pl.BlockSpec((tm, dh), lambda i: (i, 0))],
        out_specs=[pl.BlockSpec((GQA_HEADS, dh, tm), lambda i: (0, 0, i)),
                   pl.BlockSpec((GQA_KV_HEADS, tm, dh), lambda i: (0, i, 0)),
                   pl.BlockSpec((GQA_KV_HEADS, dh + VT_PAD, tm), lambda i: (0, 0, i))],
        out_shape=[jax.ShapeDtypeStruct((GQA_HEADS, dh, m), BF16),
                   jax.ShapeDtypeStruct((GQA_KV_HEADS, m, dh), BF16),
                   jax.ShapeDtypeStruct((GQA_KV_HEADS, dh + VT_PAD, m), BF16)],
        compiler_params=_cparams(("parallel",)),
    )(z, gq.reshape(1, dh), gk.reshape(1, dh), cosf, sins)


ATTN_TQ = 256
VT_PAD = 16


def _vt_pad_rows(t):
    row = lax.broadcasted_iota(jnp.int32, (VT_PAD, t), 0)
    return jnp.where(row == 0, 1.0, 0.0).astype(BF16)


def _attn_kernel(*refs, tk, n_lat_chunks, chains, dv):
    if n_lat_chunks:
        qt_ref, kl_ref, vlt_ref, kc_ref, vct_ref, o_ref, m_ref, acc_ref, s_ref, mc_ref = refs
    else:
        qt_ref, kc_ref, vct_ref, o_ref, m_ref, acc_ref, s_ref, mc_ref = refs
    nc = len(chains)
    m_ref[...] = jnp.full_like(m_ref, -1e30)
    acc_ref[...] = jnp.zeros_like(acc_ref)

    def qk(slot, k, n):
        g, q0 = chains[n]
        rows = k.shape[0]
        st = _dot(k, qt_ref[g, :, q0:q0 + ATTN_TQ])
        s_ref[slot, n, 0:rows, :] = st
        mc_ref[slot, n] = jnp.max(st, axis=0, keepdims=True)

    def sm_pv(slot, vt, n):
        rows = vt.shape[1]
        m_old = m_ref[n]
        m_new = jnp.maximum(m_old, mc_ref[slot, n])
        p = jnp.exp2((s_ref[slot, n, 0:rows, :] - m_new).astype(BF16))
        alpha = jnp.exp2(m_old - m_new)
        m_ref[n] = m_new
        acc_ref[n] = alpha * acc_ref[n] + _dot(vt, p)

    def stage(slot_s, k, slot_f, vt):
        if k is not None:
            qk(slot_s, k, 0)
        for n in range(nc):
            if k is not None and n + 1 < nc:
                qk(slot_s, k, n + 1)
            if vt is not None:
                sm_pv(slot_f, vt, n)

    def lat_k(c):
        return kl_ref[0, pl.ds(pl.multiple_of(c * tk, tk), tk), :]

    def lat_vt(c):
        return vlt_ref[0, :, pl.ds(pl.multiple_of(c * tk, tk), tk)]

    stage(1, kc_ref[0], None, None)
    if n_lat_chunks:
        stage(0, lat_k(0), 1, vct_ref[0])

        def body(i, carry):
            c = 2 * i
            stage(1, lat_k(c + 1), 0, lat_vt(c))
            stage(0, lat_k(c + 2), 1, lat_vt(c + 1))
            return carry
        lax.fori_loop(0, n_lat_chunks // 2 - 1, body, 0)
        c = n_lat_chunks - 2
        stage(1, lat_k(c + 1), 0, lat_vt(c))
        stage(None, None, 1, lat_vt(c + 1))
    else:
        stage(None, None, 1, vct_ref[0])

    for n, (g, q0) in enumerate(chains):
        o = acc_ref[n, 0:dv, :] / acc_ref[n, dv:dv + 1, :]
        o_ref[q0:q0 + ATTN_TQ, g * dv:(g + 1) * dv] = o.T.astype(o_ref.dtype)


def attention(qt, k_lat, vt_lat, k_ctx, vt_ctx):
    hq, dq, lq = qt.shape
    hk, lc, _ = k_ctx.shape
    dva = vt_ctx.shape[1]
    dv = dva - VT_PAD
    grp = hq // hk
    n_chains = 8
    tq = min(ATTN_TQ * max(n_chains // grp, 1), lq)
    chains = [(g, q0) for g in range(grp) for q0 in range(0, tq, ATTN_TQ)]
    if k_lat is not None:
        ll = k_lat.shape[1]
        tk = min(1024, ll // 2)
        n_chunks = ll // tk
        assert n_chunks % 2 == 0 and tk >= lc
    else:
        tk, n_chunks = lc, 0
    in_specs = [pl.BlockSpec((grp, dq, tq), lambda h, i: (h, 0, i))]
    args = [qt]
    if k_lat is not None:
        in_specs += [pl.BlockSpec((1, ll, dq), lambda h, i: (h, 0, 0)),
                     pl.BlockSpec((1, dva, ll), lambda h, i: (h, 0, 0))]
        args += [k_lat, vt_lat]
    in_specs += [pl.BlockSpec((1, lc, dq), lambda h, i: (h, 0, 0)),
                 pl.BlockSpec((1, dva, lc), lambda h, i: (h, 0, 0))]
    args += [k_ctx, vt_ctx]
    nc = len(chains)
    return pl.pallas_call(
        functools.partial(_attn_kernel, tk=tk, n_lat_chunks=n_chunks, chains=chains, dv=dv),
        grid=(hk, lq // tq),
        in_specs=in_specs,
        out_specs=pl.BlockSpec((tq, grp * dv), lambda h, i: (i, h)),
        out_shape=jax.ShapeDtypeStruct((lq, hq * dv), BF16),
        scratch_shapes=[pltpu.VMEM((nc, 1, ATTN_TQ), F32), pltpu.VMEM((nc, dva, ATTN_TQ), F32),
                        pltpu.VMEM((2, nc, tk, ATTN_TQ), F32), pltpu.VMEM((2, nc, 1, ATTN_TQ), F32)],
        compiler_params=_cparams(("parallel", "arbitrary")),
    )(*args)


def _ret_kernel(*refs, chunk, reverse, finalize):
    if finalize:
        (lg_ref, z_ref, cos_ref, sin_ref, s0_ref, of_ref,
         o_ref, sfin_ref, s_ref, dec_ref, qd_ref, kd_ref) = refs
    else:
        (lg_ref, z_ref, cos_ref, sin_ref, s0_ref,
         o_ref, sfin_ref, s_ref, dec_ref, qd_ref, kd_ref) = refs
        of_ref = None
    i = pl.program_id(0)
    c = chunk
    dk, dvv, nh = RET_DK, RET_DV, RET_HEADS

    @pl.when(i == 0)
    def _():
        s_ref[...] = s0_ref[...]
        row = lax.broadcasted_iota(jnp.int32, (c, c), 0).astype(F32)
        col = lax.broadcasted_iota(jnp.int32, (c, c), 1).astype(F32)
        rel = (col - row) if reverse else (row - col)
        pos = lax.broadcasted_iota(jnp.int32, (c, LANES), 0).astype(F32)
        if reverse:
            pos = (c - 1.0) - pos
        for h in range(nh):
            lg = lg_ref[h]
            dec_ref[h] = jnp.where(rel >= 0, jnp.exp(lg * jnp.maximum(rel, 0.0)), 0.0)
            qd_ref[h] = jnp.exp(lg * (pos + 1.0))
            kd_ref[h] = jnp.exp(lg * ((c - 1.0) - pos))

    cosf = cos_ref[...]
    sins = sin_ref[...]
    q0, k0, v0, g0 = 0, nh * dk, 2 * nh * dk, 2 * nh * dk + nh * dvv
    for h in range(nh):
        q = _rope128(z_ref[:, q0 + h * dk:q0 + (h + 1) * dk].astype(F32), cosf, sins)
        k = _rope128(z_ref[:, k0 + h * dk:k0 + (h + 1) * dk].astype(F32) * (dk ** -0.5), cosf, sins)
        v = z_ref[:, v0 + h * dvv:v0 + (h + 1) * dvv]
        s = s_ref[h]
        att = _dot_nt(q.astype(BF16), k.astype(BF16)) * dec_ref[h]
        o = _dot(att.astype(BF16), v) + _dot((q * qd_ref[h]).astype(BF16), s.astype(BF16))
        kdt = (k * kd_ref[h]).T.astype(BF16)
        cdec = jnp.exp(jnp.full((1, 1), c, F32) * lg_ref[h])
        s_ref[h] = cdec * s + _dot(kdt, v)
        if finalize:
            y = _rms(o + of_ref[:, h * dvv:(h + 1) * dvv])
            g = z_ref[:, g0 + h * dvv:g0 + (h + 1) * dvv].astype(F32)
            o_ref[:, h * dvv:(h + 1) * dvv] = (y * (g * jax.nn.sigmoid(g))).astype(o_ref.dtype)
        else:
            o_ref[:, h * dvv:(h + 1) * dvv] = o

    @pl.when(i == pl.num_programs(0) - 1)
    def _():
        sfin_ref[...] = s_ref[...]


def retention(z, zblk, lg, rope, s0, o_fwd=None):
    m = z.shape[0]
    c = min(256, m)
    n = m // c
    reverse = o_fwd is not None
    cosf, sins = rope
    nh, dk, dvv = RET_HEADS, RET_DK, RET_DV
    blk = (lambda i: (n - 1 - i, 0)) if reverse else (lambda i: (i, 0))
    zmap = (lambda i: (n - 1 - i, zblk)) if reverse else (lambda i: (i, zblk))
    in_specs = [pl.BlockSpec(memory_space=pltpu.SMEM),
                pl.BlockSpec((c, 3072), zmap),
                pl.BlockSpec((c, LANES), blk),
                pl.BlockSpec((c, LANES), blk),
                pl.BlockSpec((nh, dk, dvv), lambda i: (0, 0, 0))]
    args = [lg, z, cosf, sins, s0]
    if reverse:
        in_specs.append(pl.BlockSpec((c, nh * dvv), blk))
        args.append(o_fwd)
    return pl.pallas_call(
        functools.partial(_ret_kernel, chunk=c, reverse=reverse, finalize=reverse),
        grid=(n,),
        in_specs=in_specs,
        out_specs=[pl.BlockSpec((c, nh * dvv), blk),
                   pl.BlockSpec((nh, dk, dvv), lambda i: (0, 0, 0))],
        out_shape=[jax.ShapeDtypeStruct((m, nh * dvv), BF16 if reverse else F32),
                   jax.ShapeDtypeStruct((nh, dk, dvv), F32)],
        scratch_shapes=[pltpu.VMEM((nh, dk, dvv), F32), pltpu.VMEM((nh, c, c), F32),
                        pltpu.VMEM((nh, c, LANES), F32), pltpu.VMEM((nh, c, LANES), F32)],
        compiler_params=_cparams(("arbitrary",)),
    )(*args)


def _merge_kernel(x_ref, g0_ref, g1_ref, g2_ref, oa_ref, ob_ref, oc_ref, wa_ref, wb_ref, wc_ref,
                  wo_ref, mod_ref, o_ref):
    y = jax.nn.sigmoid(g0_ref[...].astype(F32)) * _dot(oa_ref[...], wa_ref[...])
    y += jax.nn.sigmoid(g1_ref[...].astype(F32)) * _dot(ob_ref[...], wb_ref[...])
    y += jax.nn.sigmoid(g2_ref[...].astype(F32)) * _dot(oc_ref[...], wc_ref[...])
    out = _dot(y.astype(BF16), wo_ref[...])
    o_ref[...] = x_ref[...] + mod_ref[...] * out


def merge(x, z, oa, ob, oc, wa, wb, wc, wo, gate):
    m, d = x.shape
    tm = min(256, m)
    const = lambda i: (0, 0)
    wspec = lambda w: pl.BlockSpec(w.shape, const, pipeline_mode=pl.Buffered(1))
    return pl.pallas_call(
        _merge_kernel,
        grid=(m // tm,),
        in_specs=[pl.BlockSpec((tm, d), lambda i: (i, 0)),
                  pl.BlockSpec((tm, d), lambda i: (i, 0)),
                  pl.BlockSpec((tm, d), lambda i: (i, 1)),
                  pl.BlockSpec((tm, d), lambda i: (i, 2)),
                  pl.BlockSpec((tm, oa.shape[1]), lambda i: (i, 0)),
                  pl.BlockSpec((tm, ob.shape[1]), lambda i: (i, 0)),
                  pl.BlockSpec((tm, oc.shape[1]), lambda i: (i, 0)),
                  wspec(wa), wspec(wb), wspec(wc), wspec(wo),
                  pl.BlockSpec((1, d), const)],
        out_specs=pl.BlockSpec((tm, d), lambda i: (i, 0)),
        out_shape=jax.ShapeDtypeStruct((m, d), F32),
        compiler_params=_cparams(("parallel",)),
    )(x, z, z, z, oa, ob, oc, wa, wb, wc, wo, gate.reshape(1, d))


def _rope_tables(n_tok, n_ctx):
    n_rows = n_tok // GRID_W
    rows = jnp.repeat(jnp.arange(n_rows, dtype=F32), GRID_W)
    cols = jnp.tile(jnp.arange(GRID_W, dtype=F32), n_rows)

    def table(dim):
        d_axis = dim // 2
        inv_freq = ROPE_THETA ** (-jnp.arange(0, d_axis, 2, dtype=F32) / d_axis)
        ang = jnp.concatenate([rows[:, None] * inv_freq, cols[:, None] * inv_freq], axis=-1)
        return jnp.cos(ang), jnp.sin(ang)

    def mla_pack(cos, sin):
        t = cos.shape[0]
        z32 = jnp.zeros((t, 32), F32)
        z64 = jnp.zeros((t, 64), F32)
        return (cos.T, sin.T,
                jnp.concatenate([cos, cos, z64], axis=1),
                jnp.concatenate([z32, sin, z64], axis=1),
                jnp.concatenate([-sin, z32, z64], axis=1))

    def pack128(cos, sin):
        return (jnp.concatenate([cos, cos], axis=1), jnp.concatenate([-sin, sin], axis=1))

    cm, sm = table(MLA_ROPE)
    c128, s128 = table(GQA_HEAD_DIM)
    lat = (mla_pack(cm, sm), pack128(c128, s128))
    one = lambda w: jnp.ones((n_ctx, w), F32)
    zero = lambda w: jnp.zeros((n_ctx, w), F32)
    ctx = (mla_pack(one(MLA_ROPE // 2), zero(MLA_ROPE // 2)), pack128(one(64), zero(64)))
    return lat, ctx


def _layer_weights(p, d):
    w = p['mix_w_in']
    sizes = (MLA_Q_LORA, MLA_KV_LORA, MLA_ROPE, GQA_HEADS * GQA_HEAD_DIM, GQA_KV_HEADS * GQA_HEAD_DIM,
             GQA_KV_HEADS * GQA_HEAD_DIM, RET_HEADS * RET_DK, RET_HEADS * RET_DK, RET_HEADS * RET_DV,
             RET_HEADS * RET_DV, 3 * d)
    offs = np.concatenate([[0], np.cumsum(sizes)])
    part = [w[:, offs[i]:offs[i + 1]] for i in range(len(sizes))]
    used = 3 * d + 3072 + 1536 + MLA_Q_LORA + MLA_KV_LORA + MLA_ROPE
    w_z = jnp.concatenate([part[10], part[6], part[7], part[8], part[9], part[3], part[4], part[5],
                           part[0], part[1], part[2], jnp.zeros((d, Z_COLS - used), w.dtype)], axis=1)
    hh = MLA_HEADS
    wq = p['mla_w_uq'].reshape(MLA_Q_LORA, hh, MLA_NOPE + MLA_ROPE)
    wq = jnp.concatenate([wq, jnp.zeros((MLA_Q_LORA, hh, MLA_QK - MLA_NOPE - MLA_ROPE), wq.dtype)], axis=2)
    wqt = wq.transpose(1, 2, 0)
    wkv = p['mla_w_ukv'].reshape(MLA_KV_LORA, hh, MLA_NOPE + MLA_VDIM)
    wk = wkv[:, :, :MLA_NOPE].reshape(MLA_KV_LORA, hh * MLA_NOPE)
    wvt = wkv[:, :, MLA_NOPE:].reshape(MLA_KV_LORA, hh * MLA_VDIM).T
    bf = lambda a: a.astype(BF16)
    return dict(w_z=bf(w_z), wqt=bf(wqt), wk=bf(wk), wvt=bf(wvt),
                wa=bf(p['branch_w_mla']), wb=bf(p['branch_w_gqa']), wc=bf(p['branch_w_ret']),
                wo=bf(p['mix_w_out']),
                ffn1_in=bf(p['ffn1_w_in']), ffn1_out=bf(p['ffn1_w_out']),
                ffn2_in=bf(p['ffn2_w_in']), ffn2_out=bf(p['ffn2_w_out']))


def _mixer(x, xc, m, mc, p, w, ropes, need_ctx):
    (rl_mla, rl_128), (rc_mla, rc_128) = ropes
    d = x.shape[1]
    zl = inproj(x, p['norm_mix'], m[3:5], w['w_z'])
    zc = inproj(xc, p['norm_mix'], mc[3:5], w['w_z'])
    ret_blk, gqa_blk, mla_blk = (3 * d) // 3072, (3 * d + 3072) // 1536, (3 * d + 3072 + 1536) // 1536
    qa_l, ka_l, va_l = mla_prep(zl, mla_blk, p['mla_q_norm'], p['mla_kv_norm'], w['wqt'], w['wk'], w['wvt'], rl_mla)
    qa_c, ka_c, va_c = mla_prep(zc, mla_blk, p['mla_q_norm'], p['mla_kv_norm'], w['wqt'], w['wk'], w['wvt'], rc_mla)
    oa_l = attention(qa_l, ka_l, va_l, ka_c, va_c)
    qb_l, kb_l, vb_l = gqa_prep(zl, gqa_blk, p['gqa_q_norm'], p['gqa_k_norm'], rl_128)
    qb_c, kb_c, vb_c = gqa_prep(zc, gqa_blk, p['gqa_q_norm'], p['gqa_k_norm'], rc_128)
    ob_l = attention(qb_l, kb_l, vb_l, kb_c, vb_c)
    lg_f, lg_b = p['ret_log_decay'][0], p['ret_log_decay'][1]
    s0 = jnp.zeros((RET_HEADS, RET_DK, RET_DV), F32)
    or_cf, st_f = retention(zc, ret_blk, lg_f, rc_128, s0)
    or_c, st_b = retention(zc, ret_blk, lg_b, rc_128, s0, o_fwd=or_cf)
    or_lf, _ = retention(zl, ret_blk, lg_f, rl_128, st_f)
    or_l, _ = retention(zl, ret_blk, lg_b, rl_128, st_b, o_fwd=or_lf)
    x = merge(x, zl, oa_l, ob_l, or_l, w['wa'], w['wb'], w['wc'], w['wo'], m[5])
    if need_ctx:
        oa_c = attention(qa_c, None, None, ka_c, va_c)
        ob_c = attention(qb_c, None, None, kb_c, vb_c)
        xc = merge(xc, zc, oa_c, ob_c, or_c, w['wa'], w['wb'], w['wc'], w['wo'], mc[5])
    return x, xc


def kernel(x, c, ctx, c_ctx, mod_w, mod_b, norm_ffn1, ffn1_w_in, ffn1_w_out, norm_mix, mix_w_in, mla_q_norm, mla_w_uq, mla_kv_norm, mla_w_ukv, gqa_q_norm, gqa_k_norm, ret_log_decay, branch_w_mla, branch_w_gqa, branch_w_ret, mix_w_out, norm_ffn2, ffn2_w_in, ffn2_w_out, final_norm):
    assert x.shape[0] == 1 and ctx.shape[0] == 1 and c.shape[0] == 1
    depth = mod_w.shape[0]
    d = x.shape[2]
    n_tok, n_ctx = x.shape[1], ctx.shape[1]
    ropes = _rope_tables(n_tok, n_ctx)
    xs, xc = x[0], ctx[0]
    cvecs = jnp.concatenate([c, c_ctx[None]], axis=0)
    for layer in range(depth):
        p = {
            'norm_mix': norm_mix[layer], 'mix_w_in': mix_w_in[layer],
            'mla_q_norm': mla_q_norm[layer], 'mla_w_uq': mla_w_uq[layer],
            'mla_kv_norm': mla_kv_norm[layer], 'mla_w_ukv': mla_w_ukv[layer],
            'gqa_q_norm': gqa_q_norm[layer], 'gqa_k_norm': gqa_k_norm[layer],
            'ret_log_decay': ret_log_decay[layer],
            'branch_w_mla': branch_w_mla[layer], 'branch_w_gqa': branch_w_gqa[layer],
            'branch_w_ret': branch_w_ret[layer], 'mix_w_out': mix_w_out[layer],
            'ffn1_w_in': ffn1_w_in[layer], 'ffn1_w_out': ffn1_w_out[layer],
            'ffn2_w_in': ffn2_w_in[layer], 'ffn2_w_out': ffn2_w_out[layer],
        }
        w = _layer_weights(p, d)
        need_ctx = layer < depth - 1
        mods = modvec(cvecs, mod_w[layer], mod_b[layer]).reshape(2, N_MOD, d)
        m, mc = mods[0], mods[1]
        xs = ffn(xs, norm_ffn1[layer], m[0:3], w['ffn1_in'], w['ffn1_out'])
        xc = ffn(xc, norm_ffn1[layer], mc[0:3], w['ffn1_in'], w['ffn1_out'])
        xs, xc = _mixer(xs, xc, m, mc, p, w, ropes, need_ctx)
        last = layer == depth - 1
        xs = ffn(xs, norm_ffn2[layer], m[6:9], w['ffn2_in'], w['ffn2_out'], fin=final_norm if last else None)
        if need_ctx:
            xc = ffn(xc, norm_ffn2[layer], mc[6:9], w['ffn2_in'], w['ffn2_out'])
    return xs[None]
```

```python
import functools
import math

import numpy as np
import jax
import jax.numpy as jnp
from jax import lax
from jax.experimental import pallas as pl
from jax.experimental.pallas import tpu as pltpu

F32 = jnp.float32
BF16 = jnp.bfloat16

GRID_W = 64
N_MOD = 9
MLA_HEADS = 8
MLA_Q_LORA = 512
MLA_KV_LORA = 512
MLA_NOPE = 128
MLA_ROPE = 64
MLA_VDIM = 128
GQA_HEADS = 8
GQA_KV_HEADS = 2
GQA_HEAD_DIM = 128
RET_HEADS = 4
RET_DK = 128
RET_DV = 256
ROPE_THETA = 10000.0
NORM_EPS = 1e-6
LOG2E = math.log2(math.e)
MLA_QSCALE = (MLA_NOPE + MLA_ROPE) ** -0.5 * LOG2E
GQA_QSCALE = GQA_HEAD_DIM ** -0.5 * LOG2E
MLA_QK = 256

LANES = 128
V7X_VMEM_LIMIT = 56 * 1024 * 1024

FFN_TM = 512
INPROJ_TM = 1024

Z_GATE0 = 0
Z_COLS = 12288


def _cparams(sem):
    return pltpu.CompilerParams(dimension_semantics=sem, vmem_limit_bytes=V7X_VMEM_LIMIT)


def _rms(x):
    return x * lax.rsqrt(jnp.mean(x * x, axis=-1, keepdims=True) + NORM_EPS)


def _dot(a, b):
    return jnp.dot(a, b, preferred_element_type=F32)


def _dot_nt(a, b):
    return lax.dot_general(a, b, (((1,), (1,)), ((), ())), preferred_element_type=F32)


def _modvec_kernel(cb_ref, w_ref, b_ref, o_ref):
    tn = w_ref.shape[1]
    d = w_ref.shape[0]
    for r in range(2):
        cb = cb_ref[r]
        s = cb * jax.nn.sigmoid(cb)
        for g in range(tn // LANES):
            w = w_ref[:, g * LANES:(g + 1) * LANES]
            part = jnp.sum((w * s).reshape(d // 8, 8, LANES), axis=0)
            o_ref[r:r + 1, g * LANES:(g + 1) * LANES] = (
                jnp.sum(part, axis=0, keepdims=True) + b_ref[:, g * LANES:(g + 1) * LANES])


def modvec(cvecs, w, b, layer):
    _, d, n = w.shape
    tn = 1024
    cb = jnp.broadcast_to(cvecs[:, :, None], (2, d, LANES))
    return pl.pallas_call(
        _modvec_kernel,
        grid=(n // tn,),
        in_specs=[pl.BlockSpec((2, d, LANES), lambda j: (0, 0, 0)),
                  pl.BlockSpec((None, d, tn), lambda j: (layer, 0, j)),
                  pl.BlockSpec((None, 1, tn), lambda j: (layer, 0, j))],
        out_specs=pl.BlockSpec((2, tn), lambda j: (0, j)),
        out_shape=jax.ShapeDtypeStruct((2, n), F32),
        compiler_params=_cparams(("arbitrary",)),
    )(cb, w, b.reshape(b.shape[0], 1, n))


def _ffn_kernel(x_ref, g_ref, mod_ref, wa_ref, wb_ref, wo_ref, fin_ref, o_ref, h_ref, *, final_norm):
    j = pl.program_id(1)

    @pl.when(j == 0)
    def _():
        y = _rms(x_ref[...]) * g_ref[...]
        h = y * (1.0 + mod_ref[1:2, :]) + mod_ref[0:1, :]
        h_ref[...] = h.astype(BF16)
        o_ref[...] = jnp.zeros_like(o_ref)

    h = h_ref[...]
    a = _dot(h, wa_ref[...])
    b = _dot(h, wb_ref[...])
    act = (a * jax.nn.sigmoid(a) * b).astype(BF16)
    o_ref[...] += _dot(act, wo_ref[...])

    @pl.when(j == pl.num_programs(1) - 1)
    def _():
        xn = x_ref[...] + (0.5 * mod_ref[2:3, :]) * o_ref[...]
        if final_norm:
            xn = _rms(xn) * fin_ref[...]
        o_ref[...] = xn


def ffn(x, gain, mod3, w_in, w_out, layer, fin=None):
    m, d = x.shape
    f = w_out.shape[1]
    tm = min(FFN_TM, m)
    tf = 512
    nf = f // tf
    final_norm = fin is not None
    if fin is None:
        fin = jnp.ones((d,), F32)
    return pl.pallas_call(
        functools.partial(_ffn_kernel, final_norm=final_norm),
        grid=(m // tm, nf),
        in_specs=[pl.BlockSpec((tm, d), lambda i, j: (i, 0)),
                  pl.BlockSpec((1, d), lambda i, j: (0, 0)),
                  pl.BlockSpec((3, d), lambda i, j: (0, 0)),
                  pl.BlockSpec((None, d, tf), lambda i, j: (layer, 0, j)),
                  pl.BlockSpec((None, d, tf), lambda i, j: (layer, 0, j + nf)),
                  pl.BlockSpec((None, tf, d), lambda i, j: (layer, j, 0)),
                  pl.BlockSpec((1, d), lambda i, j: (0, 0))],
        out_specs=pl.BlockSpec((tm, d), lambda i, j: (i, 0)),
        out_shape=jax.ShapeDtypeStruct((m, d), F32),
        scratch_shapes=[pltpu.VMEM((tm, d), BF16)],
        compiler_params=_cparams(("parallel", "arbitrary")),
    )(x, gain.reshape(1, d), mod3, w_in, w_in, w_out, fin.reshape(1, d))


def _inproj_kernel(x_ref, g_ref, mod_ref, w_ref, o_ref, h_ref):
    @pl.when(pl.program_id(1) == 0)
    def _():
        y = _rms(x_ref[...]) * g_ref[...]
        h_ref[...] = (y * (1.0 + mod_ref[1:2, :]) + mod_ref[0:1, :]).astype(BF16)

    o_ref[...] = _dot(h_ref[...], w_ref[...]).astype(o_ref.dtype)


def inproj(x, gain, mod2, w, layer):
    m, d = x.shape
    n = w.shape[2]
    tm = min(INPROJ_TM, m)
    tn = 1536
    return pl.pallas_call(
        _inproj_kernel,
        grid=(m // tm, n // tn),
        in_specs=[pl.BlockSpec((tm, d), lambda i, j: (i, 0)),
                  pl.BlockSpec((1, d), lambda i, j: (0, 0)),
                  pl.BlockSpec((2, d), lambda i, j: (0, 0)),
                  pl.BlockSpec((None, d, tn), lambda i, j: (layer, 0, j))],
        out_specs=pl.BlockSpec((tm, tn), lambda i, j: (i, j)),
        out_shape=jax.ShapeDtypeStruct((m, n), BF16),
        scratch_shapes=[pltpu.VMEM((tm, d), BF16)],
        compiler_params=_cparams(("parallel", "arbitrary")),
    )(x, gain.reshape(1, d), mod2, w)


def _mla_prep_kernel(z_ref, gq_ref, gkv_ref, wqt_ref, wk_ref, wvt_ref, cost_ref, sint_ref,
                     kcos_ref, ksa_ref, ksb_ref, qt_ref, kc_ref, vt_ref):
    cq = z_ref[:, 0:MLA_Q_LORA].astype(F32)
    ckv = z_ref[:, MLA_Q_LORA:MLA_Q_LORA + MLA_KV_LORA].astype(F32)
    kr = z_ref[:, MLA_Q_LORA + MLA_KV_LORA:MLA_Q_LORA + MLA_KV_LORA + LANES].astype(F32)
    cqn = (_rms(cq) * gq_ref[...]).astype(BF16)
    ckvn = (_rms(ckv) * gkv_ref[...]).astype(BF16)
    kr_r = (kr * kcos_ref[...] + pltpu.roll(kr, 32, 1) * ksa_ref[...]
            + pltpu.roll(kr, 96, 1) * ksb_ref[...]).astype(BF16)
    knope = _dot(ckvn, wk_ref[...])
    vt = _dot_nt(wvt_ref[...], ckvn)
    cos_t = cost_ref[...]
    sin_t = sint_ref[...]
    half = MLA_ROPE // 2
    for h in range(MLA_HEADS):
        qt = _dot_nt(wqt_ref[h], cqn) * MLA_QSCALE
        x1 = qt[MLA_NOPE:MLA_NOPE + half]
        x2 = qt[MLA_NOPE + half:MLA_NOPE + 2 * half]
        qt_ref[h, 0:MLA_NOPE, :] = qt[0:MLA_NOPE].astype(BF16)
        qt_ref[h, MLA_NOPE:MLA_NOPE + half, :] = (x1 * cos_t - x2 * sin_t).astype(BF16)
        qt_ref[h, MLA_NOPE + half:MLA_NOPE + 2 * half, :] = (x1 * sin_t + x2 * cos_t).astype(BF16)
        qt_ref[h, MLA_NOPE + 2 * half:, :] = qt[MLA_NOPE + 2 * half:].astype(BF16)
        kc_ref[h, :, 0:MLA_NOPE] = knope[:, h * MLA_NOPE:(h + 1) * MLA_NOPE].astype(BF16)
        kc_ref[h, :, MLA_NOPE:] = kr_r
        vt_ref[h, 0:MLA_VDIM, :] = vt[h * MLA_VDIM:(h + 1) * MLA_VDIM].astype(BF16)
        vt_ref[h, MLA_VDIM:, :] = _vt_pad_rows(vt.shape[1])


def mla_prep(z, zblk, gq, gkv, wqt, wk, wvt, layer, rope):
    m = z.shape[0]
    tm = min(512, m)
    cos_t, sin_t, kcos, ksa, ksb = rope
    hh = MLA_HEADS
    return pl.pallas_call(
        _mla_prep_kernel,
        grid=(m // tm,),
        in_specs=[pl.BlockSpec((tm, 1536), lambda i: (i, zblk)),
                  pl.BlockSpec((1, MLA_Q_LORA), lambda i: (0, 0)),
                  pl.BlockSpec((1, MLA_KV_LORA), lambda i: (0, 0)),
                  pl.BlockSpec((None, hh, MLA_QK, MLA_Q_LORA), lambda i: (layer, 0, 0, 0)),
                  pl.BlockSpec((None, MLA_KV_LORA, hh * MLA_NOPE), lambda i: (layer, 0, 0)),
                  pl.BlockSpec((None, hh * MLA_VDIM, MLA_KV_LORA), lambda i: (layer, 0, 0)),
                  pl.BlockSpec((MLA_ROPE // 2, tm), lambda i: (0, i)),
                  pl.BlockSpec((MLA_ROPE // 2, tm), lambda i: (0, i)),
                  pl.BlockSpec((tm, LANES), lambda i: (i, 0)),
                  pl.BlockSpec((tm, LANES), lambda i: (i, 0)),
                  pl.BlockSpec((tm, LANES), lambda i: (i, 0))],
        out_specs=[pl.BlockSpec((hh, MLA_QK, tm), lambda i: (0, 0, i)),
                   pl.BlockSpec((hh, tm, MLA_QK), lambda i: (0, i, 0)),
                   pl.BlockSpec((hh, MLA_VDIM + VT_PAD, tm), lambda i: (0, 0, i))],
        out_shape=[jax.ShapeDtypeStruct((hh, MLA_QK, m), BF16),
                   jax.ShapeDtypeStruct((hh, m, MLA_QK), BF16),
                   jax.ShapeDtypeStruct((hh, MLA_VDIM + VT_PAD, m), BF16)],
        compiler_params=_cparams(("parallel",)),
    )(z, gq.reshape(1, -1), gkv.reshape(1, -1), wqt, wk, wvt, cos_t, sin_t, kcos, ksa, ksb)


def _rope128(x, cosf, sins):
    return x * cosf + pltpu.roll(x, 64, 1) * sins


def _gqa_prep_kernel(z_ref, gq_ref, gk_ref, cos_ref, sin_ref, qt_ref, k_ref, vt_ref):
    cosf = cos_ref[...]
    sins = sin_ref[...]
    dh = GQA_HEAD_DIM
    for h in range(GQA_HEADS):
        x = z_ref[:, h * dh:(h + 1) * dh].astype(F32)
        q = _rope128(_rms(x) * gq_ref[...], cosf, sins) * GQA_QSCALE
        qt_ref[h] = q.T.astype(BF16)
    k0 = GQA_HEADS * dh
    v0 = k0 + GQA_KV_HEADS * dh
    for h in range(GQA_KV_HEADS):
        x = z_ref[:, k0 + h * dh:k0 + (h + 1) * dh].astype(F32)
        k_ref[h] = _rope128(_rms(x) * gk_ref[...], cosf, sins).astype(BF16)
        v = z_ref[:, v0 + h * dh:v0 + (h + 1) * dh].astype(F32)
        vt_ref[h, 0:dh, :] = v.T.astype(BF16)
        vt_ref[h, dh:, :] = _vt_pad_rows(v.shape[0])


def gqa_prep(z, zblk, gq, gk, rope):
    m = z.shape[0]
    tm = min(512, m)
    cosf, sins = rope
    dh = GQA_HEAD_DIM
    return pl.pallas_call(
        _gqa_prep_kernel,
        grid=(m // tm,),
        in_specs=[pl.BlockSpec((tm, 1536), lambda i: (i, zblk)),
                  pl.BlockSpec((1, dh), lambda i: (0, 0)),
                  pl.BlockSpec((1, dh), lambda i: (0, 0)),
                  pl.BlockSpec((tm, dh), lambda i: (i, 0)),
                  pl.BlockSpec((tm, dh), lambda i: (i, 0))],
        out_specs=[pl.BlockSpec((GQA_HEADS, dh, tm), lambda i: (0, 0, i)),
                   pl.BlockSpec((GQA_KV_HEADS, tm, dh), lambda i: (0, i, 0)),
                   pl.BlockSpec((GQA_KV_HEADS, dh + VT_PAD, tm), lambda i: (0, 0, i))],
        out_shape=[jax.ShapeDtypeStruct((GQA_HEADS, dh, m), BF16),
                   jax.ShapeDtypeStruct((GQA_KV_HEADS, m, dh), BF16),
                   jax.ShapeDtypeStruct((GQA_KV_HEADS, dh + VT_PAD, m), BF16)],
        compiler_params=_cparams(("parallel",)),
    )(z, gq.reshape(1, dh), gk.reshape(1, dh), cosf, sins)


ATTN_TQ = 256
VT_PAD = 16


def _vt_pad_rows(t):
    row = lax.broadcasted_iota(jnp.int32, (VT_PAD, t), 0)
    return jnp.where(row == 0, 1.0, 0.0).astype(BF16)


def _attn_kernel(*refs, tk, n_lat_chunks, chains, dv):
    if n_lat_chunks:
        qt_ref, kl_ref, vlt_ref, kc_ref, vct_ref, o_ref, m_ref, acc_ref, s_ref, mc_ref = refs
    else:
        qt_ref, kc_ref, vct_ref, o_ref, m_ref, acc_ref, s_ref, mc_ref = refs
    nc = len(chains)
    m_ref[...] = jnp.full_like(m_ref, -1e30)
    acc_ref[...] = jnp.zeros_like(acc_ref)

    def qk(slot, k, n):
        g, q0 = chains[n]
        rows = k.shape[0]
        st = _dot(k, qt_ref[g, :, q0:q0 + ATTN_TQ])
        s_ref[slot, n, 0:rows, :] = st
        mc_ref[slot, n] = jnp.max(st, axis=0, keepdims=True)

    def sm_pv(slot, vt, n):
        rows = vt.shape[1]
        m_old = m_ref[n]
        m_new = jnp.maximum(m_old, mc_ref[slot, n])
        p = jnp.exp2((s_ref[slot, n, 0:rows, :] - m_new).astype(BF16))
        alpha = jnp.exp2(m_old - m_new)
        m_ref[n] = m_new
        acc_ref[n] = alpha * acc_ref[n] + _dot(vt, p)

    def stage(slot_s, k, slot_f, vt):
        if k is not None:
            qk(slot_s, k, 0)
        for n in range(nc):
            if k is not None and n + 1 < nc:
                qk(slot_s, k, n + 1)
            if vt is not None:
                sm_pv(slot_f, vt, n)

    def lat_k(c):
        return kl_ref[0, pl.ds(pl.multiple_of(c * tk, tk), tk), :]

    def lat_vt(c):
        return vlt_ref[0, :, pl.ds(pl.multiple_of(c * tk, tk), tk)]

    stage(1, kc_ref[0], None, None)
    if n_lat_chunks:
        stage(0, lat_k(0), 1, vct_ref[0])

        def body(i, carry):
            c = 2 * i
            stage(1, lat_k(c + 1), 0, lat_vt(c))
            stage(0, lat_k(c + 2), 1, lat_vt(c + 1))
            return carry
        lax.fori_loop(0, n_lat_chunks // 2 - 1, body, 0)
        c = n_lat_chunks - 2
        stage(1, lat_k(c + 1), 0, lat_vt(c))
        stage(None, None, 1, lat_vt(c + 1))
    else:
        stage(None, None, 1, vct_ref[0])

    for n, (g, q0) in enumerate(chains):
        o = acc_ref[n, 0:dv, :] / acc_ref[n, dv:dv + 1, :]
        o_ref[q0:q0 + ATTN_TQ, g * dv:(g + 1) * dv] = o.T.astype(o_ref.dtype)


def attention(qt, k_lat, vt_lat, k_ctx, vt_ctx):
    hq, dq, lq = qt.shape
    hk, lc, _ = k_ctx.shape
    dva = vt_ctx.shape[1]
    dv = dva - VT_PAD
    grp = hq // hk
    n_chains = 8
    tq = min(ATTN_TQ * max(n_chains // grp, 1), lq)
    chains = [(g, q0) for g in range(grp) for q0 in range(0, tq, ATTN_TQ)]
    if k_lat is not None:
        ll = k_lat.shape[1]
        tk = min(1024, ll // 2)
        n_chunks = ll // tk
        assert n_chunks % 2 == 0 and tk >= lc
    else:
        tk, n_chunks = lc, 0
    in_specs = [pl.BlockSpec((grp, dq, tq), lambda h, i: (h, 0, i))]
    args = [qt]
    if k_lat is not None:
        in_specs += [pl.BlockSpec((1, ll, dq), lambda h, i: (h, 0, 0)),
                     pl.BlockSpec((1, dva, ll), lambda h, i: (h, 0, 0))]
        args += [k_lat, vt_lat]
    in_specs += [pl.BlockSpec((1, lc, dq), lambda h, i: (h, 0, 0)),
                 pl.BlockSpec((1, dva, lc), lambda h, i: (h, 0, 0))]
    args += [k_ctx, vt_ctx]
    nc = len(chains)
    return pl.pallas_call(
        functools.partial(_attn_kernel, tk=tk, n_lat_chunks=n_chunks, chains=chains, dv=dv),
        grid=(hk, lq // tq),
        in_specs=in_specs,
        out_specs=pl.BlockSpec((tq, grp * dv), lambda h, i: (i, h)),
        out_shape=jax.ShapeDtypeStruct((lq, hq * dv), BF16),
        scratch_shapes=[pltpu.VMEM((nc, 1, ATTN_TQ), F32), pltpu.VMEM((nc, dva, ATTN_TQ), F32),
                        pltpu.VMEM((2, nc, tk, ATTN_TQ), F32), pltpu.VMEM((2, nc, 1, ATTN_TQ), F32)],
        compiler_params=_cparams(("parallel", "arbitrary")),
    )(*args)


def _ret_kernel(*refs, chunk, reverse, finalize):
    if finalize:
        (lg_ref, z_ref, cos_ref, sin_ref, s0_ref, of_ref,
         o_ref, sfin_ref, s_ref, dec_ref, qd_ref, kd_ref) = refs
    else:
        (lg_ref, z_ref, cos_ref, sin_ref, s0_ref,
         o_ref, sfin_ref, s_ref, dec_ref, qd_ref, kd_ref) = refs
        of_ref = None
    i = pl.program_id(0)
    c = chunk
    dk, dvv, nh = RET_DK, RET_DV, RET_HEADS

    @pl.when(i == 0)
    def _():
        s_ref[...] = s0_ref[...]
        row = lax.broadcasted_iota(jnp.int32, (c, c), 0).astype(F32)
        col = lax.broadcasted_iota(jnp.int32, (c, c), 1).astype(F32)
        rel = (col - row) if reverse else (row - col)
        pos = lax.broadcasted_iota(jnp.int32, (c, LANES), 0).astype(F32)
        if reverse:
            pos = (c - 1.0) - pos
        for h in range(nh):
            lg = lg_ref[h]
            dec_ref[h] = jnp.where(rel >= 0, jnp.exp(lg * jnp.maximum(rel, 0.0)), 0.0)
            qd_ref[h] = jnp.exp(lg * (pos + 1.0))
            kd_ref[h] = jnp.exp(lg * ((c - 1.0) - pos))

    cosf = cos_ref[...]
    sins = sin_ref[...]
    q0, k0, v0, g0 = 0, nh * dk, 2 * nh * dk, 2 * nh * dk + nh * dvv
    for h in range(nh):
        q = _rope128(z_ref[:, q0 + h * dk:q0 + (h + 1) * dk].astype(F32), cosf, sins)
        k = _rope128(z_ref[:, k0 + h * dk:k0 + (h + 1) * dk].astype(F32) * (dk ** -0.5), cosf, sins)
        v = z_ref[:, v0 + h * dvv:v0 + (h + 1) * dvv]
        s = s_ref[h]
        att = _dot_nt(q.astype(BF16), k.astype(BF16)) * dec_ref[h]
        o = _dot(att.astype(BF16), v) + _dot((q * qd_ref[h]).astype(BF16), s.astype(BF16))
        kdt = (k * kd_ref[h]).T.astype(BF16)
        cdec = jnp.exp(jnp.full((1, 1), c, F32) * lg_ref[h])
        s_ref[h] = cdec * s + _dot(kdt, v)
        if finalize:
            y = _rms(o + of_ref[:, h * dvv:(h + 1) * dvv])
            g = z_ref[:, g0 + h * dvv:g0 + (h + 1) * dvv].astype(F32)
            o_ref[:, h * dvv:(h + 1) * dvv] = (y * (g * jax.nn.sigmoid(g))).astype(o_ref.dtype)
        else:
            o_ref[:, h * dvv:(h + 1) * dvv] = o

    @pl.when(i == pl.num_programs(0) - 1)
    def _():
        sfin_ref[...] = s_ref[...]


def retention(z, zblk, lg, rope, s0, o_fwd=None):
    m = z.shape[0]
    c = min(256, m)
    n = m // c
    reverse = o_fwd is not None
    cosf, sins = rope
    nh, dk, dvv = RET_HEADS, RET_DK, RET_DV
    blk = (lambda i: (n - 1 - i, 0)) if reverse else (lambda i: (i, 0))
    zmap = (lambda i: (n - 1 - i, zblk)) if reverse else (lambda i: (i, zblk))
    in_specs = [pl.BlockSpec(memory_space=pltpu.SMEM),
                pl.BlockSpec((c, 3072), zmap),
                pl.BlockSpec((c, LANES), blk),
                pl.BlockSpec((c, LANES), blk),
                pl.BlockSpec((nh, dk, dvv), lambda i: (0, 0, 0))]
    args = [lg, z, cosf, sins, s0]
    if reverse:
        in_specs.append(pl.BlockSpec((c, nh * dvv), blk))
        args.append(o_fwd)
    return pl.pallas_call(
        functools.partial(_ret_kernel, chunk=c, reverse=reverse, finalize=reverse),
        grid=(n,),
        in_specs=in_specs,
        out_specs=[pl.BlockSpec((c, nh * dvv), blk),
                   pl.BlockSpec((nh, dk, dvv), lambda i: (0, 0, 0))],
        out_shape=[jax.ShapeDtypeStruct((m, nh * dvv), BF16 if reverse else F32),
                   jax.ShapeDtypeStruct((nh, dk, dvv), F32)],
        scratch_shapes=[pltpu.VMEM((nh, dk, dvv), F32), pltpu.VMEM((nh, c, c), F32),
                        pltpu.VMEM((nh, c, LANES), F32), pltpu.VMEM((nh, c, LANES), F32)],
        compiler_params=_cparams(("arbitrary",)),
    )(*args)


def _merge_kernel(x_ref, g0_ref, g1_ref, g2_ref, oa_ref, ob_ref, oc_ref, wa_ref, wb_ref, wc_ref,
                  wo_ref, mod_ref, o_ref):
    y = jax.nn.sigmoid(g0_ref[...].astype(F32)) * _dot(oa_ref[...], wa_ref[...])
    y += jax.nn.sigmoid(g1_ref[...].astype(F32)) * _dot(ob_ref[...], wb_ref[...])
    y += jax.nn.sigmoid(g2_ref[...].astype(F32)) * _dot(oc_ref[...], wc_ref[...])
    out = _dot(y.astype(BF16), wo_ref[...])
    o_ref[...] = x_ref[...] + mod_ref[...] * out


def merge(x, z, oa, ob, oc, wa, wb, wc, wo, layer, gate):
    m, d = x.shape
    tm = min(256, m)
    const = lambda i: (0, 0)
    wspec = lambda w: pl.BlockSpec((None,) + w.shape[1:], lambda i: (layer, 0, 0),
                                   pipeline_mode=pl.Buffered(1))
    return pl.pallas_call(
        _merge_kernel,
        grid=(m // tm,),
        in_specs=[pl.BlockSpec((tm, d), lambda i: (i, 0)),
                  pl.BlockSpec((tm, d), lambda i: (i, 0)),
                  pl.BlockSpec((tm, d), lambda i: (i, 1)),
                  pl.BlockSpec((tm, d), lambda i: (i, 2)),
                  pl.BlockSpec((tm, oa.shape[1]), lambda i: (i, 0)),
                  pl.BlockSpec((tm, ob.shape[1]), lambda i: (i, 0)),
                  pl.BlockSpec((tm, oc.shape[1]), lambda i: (i, 0)),
                  wspec(wa), wspec(wb), wspec(wc), wspec(wo),
                  pl.BlockSpec((1, d), const)],
        out_specs=pl.BlockSpec((tm, d), lambda i: (i, 0)),
        out_shape=jax.ShapeDtypeStruct((m, d), F32),
        compiler_params=_cparams(("parallel",)),
    )(x, z, z, z, oa, ob, oc, wa, wb, wc, wo, gate.reshape(1, d))


def _rope_tables(n_tok, n_ctx):
    n_rows = n_tok // GRID_W
    rows = jnp.repeat(jnp.arange(n_rows, dtype=F32), GRID_W)
    cols = jnp.tile(jnp.arange(GRID_W, dtype=F32), n_rows)

    def table(dim):
        d_axis = dim // 2
        inv_freq = ROPE_THETA ** (-jnp.arange(0, d_axis, 2, dtype=F32) / d_axis)
        ang = jnp.concatenate([rows[:, None] * inv_freq, cols[:, None] * inv_freq], axis=-1)
        return jnp.cos(ang), jnp.sin(ang)

    def mla_pack(cos, sin):
        t = cos.shape[0]
        z32 = jnp.zeros((t, 32), F32)
        z64 = jnp.zeros((t, 64), F32)
        return (cos.T, sin.T,
                jnp.concatenate([cos, cos, z64], axis=1),
                jnp.concatenate([z32, sin, z64], axis=1),
                jnp.concatenate([-sin, z32, z64], axis=1))

    def pack128(cos, sin):
        return (jnp.concatenate([cos, cos], axis=1), jnp.concatenate([-sin, sin], axis=1))

    cm, sm = table(MLA_ROPE)
    c128, s128 = table(GQA_HEAD_DIM)
    lat = (mla_pack(cm, sm), pack128(c128, s128))
    one = lambda w: jnp.ones((n_ctx, w), F32)
    zero = lambda w: jnp.zeros((n_ctx, w), F32)
    ctx = (mla_pack(one(MLA_ROPE // 2), zero(MLA_ROPE // 2)), pack128(one(64), zero(64)))
    return lat, ctx


def _stacked_weights(mix_w_in, mla_w_uq, mla_w_ukv, d):
    w = mix_w_in
    nl = w.shape[0]
    sizes = (MLA_Q_LORA, MLA_KV_LORA, MLA_ROPE, GQA_HEADS * GQA_HEAD_DIM, GQA_KV_HEADS * GQA_HEAD_DIM,
             GQA_KV_HEADS * GQA_HEAD_DIM, RET_HEADS * RET_DK, RET_HEADS * RET_DK, RET_HEADS * RET_DV,
             RET_HEADS * RET_DV, 3 * d)
    offs = np.concatenate([[0], np.cumsum(sizes)])
    part = [w[:, :, offs[i]:offs[i + 1]].astype(BF16) for i in range(len(sizes))]
    used = 3 * d + 3072 + 1536 + MLA_Q_LORA + MLA_KV_LORA + MLA_ROPE
    w_z = jnp.concatenate([part[10], part[6], part[7], part[8], part[9], part[3], part[4], part[5],
                           part[0], part[1], part[2], jnp.zeros((nl, d, Z_COLS - used), BF16)], axis=2)
    hh = MLA_HEADS
    wq = mla_w_uq.astype(BF16).reshape(nl, MLA_Q_LORA, hh, MLA_NOPE + MLA_ROPE)
    wq = jnp.concatenate([wq, jnp.zeros((nl, MLA_Q_LORA, hh, MLA_QK - MLA_NOPE - MLA_ROPE), BF16)], axis=3)
    wqt = wq.transpose(0, 2, 3, 1)
    wkv = mla_w_ukv.astype(BF16).reshape(nl, MLA_KV_LORA, hh, MLA_NOPE + MLA_VDIM)
    wk = wkv[:, :, :, :MLA_NOPE].reshape(nl, MLA_KV_LORA, hh * MLA_NOPE)
    wvt = wkv[:, :, :, MLA_NOPE:].reshape(nl, MLA_KV_LORA, hh * MLA_VDIM).transpose(0, 2, 1)
    return dict(w_z=w_z, wqt=wqt, wk=wk, wvt=wvt)


def _mixer(x, xc, m, mc, p, w, layer, ropes, need_ctx):
    (rl_mla, rl_128), (rc_mla, rc_128) = ropes
    d = x.shape[1]
    zl = inproj(x, p['norm_mix'], m[3:5], w['w_z'], layer)
    zc = inproj(xc, p['norm_mix'], mc[3:5], w['w_z'], layer)
    ret_blk, gqa_blk, mla_blk = (3 * d) // 3072, (3 * d + 3072) // 1536, (3 * d + 3072 + 1536) // 1536
    qa_l, ka_l, va_l = mla_prep(zl, mla_blk, p['mla_q_norm'], p['mla_kv_norm'], w['wqt'], w['wk'], w['wvt'],
                                layer, rl_mla)
    qa_c, ka_c, va_c = mla_prep(zc, mla_blk, p['mla_q_norm'], p['mla_kv_norm'], w['wqt'], w['wk'], w['wvt'],
                                layer, rc_mla)
    oa_l = attention(qa_l, ka_l, va_l, ka_c, va_c)
    qb_l, kb_l, vb_l = gqa_prep(zl, gqa_blk, p['gqa_q_norm'], p['gqa_k_norm'], rl_128)
    qb_c, kb_c, vb_c = gqa_prep(zc, gqa_blk, p['gqa_q_norm'], p['gqa_k_norm'], rc_128)
    ob_l = attention(qb_l, kb_l, vb_l, kb_c, vb_c)
    lg_f, lg_b = p['ret_log_decay'][0], p['ret_log_decay'][1]
    s0 = jnp.zeros((RET_HEADS, RET_DK, RET_DV), F32)
    or_cf, st_f = retention(zc, ret_blk, lg_f, rc_128, s0)
    or_c, st_b = retention(zc, ret_blk, lg_b, rc_128, s0, o_fwd=or_cf)
    or_lf, _ = retention(zl, ret_blk, lg_f, rl_128, st_f)
    or_l, _ = retention(zl, ret_blk, lg_b, rl_128, st_b, o_fwd=or_lf)
    x = merge(x, zl, oa_l, ob_l, or_l, w['wa'], w['wb'], w['wc'], w['wo'], layer, m[5])
    if need_ctx:
        oa_c = attention(qa_c, None, None, ka_c, va_c)
        ob_c = attention(qb_c, None, None, kb_c, vb_c)
        xc = merge(xc, zc, oa_c, ob_c, or_c, w['wa'], w['wb'], w['wc'], w['wo'], layer, mc[5])
    return x, xc


def kernel(x, c, ctx, c_ctx, mod_w, mod_b, norm_ffn1, ffn1_w_in, ffn1_w_out, norm_mix, mix_w_in, mla_q_norm, mla_w_uq, mla_kv_norm, mla_w_ukv, gqa_q_norm, gqa_k_norm, ret_log_decay, branch_w_mla, branch_w_gqa, branch_w_ret, mix_w_out, norm_ffn2, ffn2_w_in, ffn2_w_out, final_norm):
    assert x.shape[0] == 1 and ctx.shape[0] == 1 and c.shape[0] == 1
    depth = mod_w.shape[0]
    d = x.shape[2]
    n_tok, n_ctx = x.shape[1], ctx.shape[1]
    ropes = _rope_tables(n_tok, n_ctx)
    xs, xc = x[0], ctx[0]
    cvecs = jnp.concatenate([c, c_ctx[None]], axis=0)
    w = _stacked_weights(mix_w_in, mla_w_uq, mla_w_ukv, d)
    bf = lambda a: a.astype(BF16)
    w.update(wa=bf(branch_w_mla), wb=bf(branch_w_gqa), wc=bf(branch_w_ret), wo=bf(mix_w_out))
    f1_in, f1_out, f2_in, f2_out = bf(ffn1_w_in), bf(ffn1_w_out), bf(ffn2_w_in), bf(ffn2_w_out)
    for layer in range(depth):
        p = {
            'norm_mix': norm_mix[layer],
            'mla_q_norm': mla_q_norm[layer], 'mla_kv_norm': mla_kv_norm[layer],
            'gqa_q_norm': gqa_q_norm[layer], 'gqa_k_norm': gqa_k_norm[layer],
            'ret_log_decay': ret_log_decay[layer],
        }
        need_ctx = layer < depth - 1
        mods = modvec(cvecs, mod_w, mod_b, layer).reshape(2, N_MOD, d)
        m, mc = mods[0], mods[1]
        xs = ffn(xs, norm_ffn1[layer], m[0:3], f1_in, f1_out, layer)
        xc = ffn(xc, norm_ffn1[layer], mc[0:3], f1_in, f1_out, layer)
        xs, xc = _mixer(xs, xc, m, mc, p, w, layer, ropes, need_ctx)
        last = layer == depth - 1
        xs = ffn(xs, norm_ffn2[layer], m[6:9], f2_in, f2_out, layer, fin=final_norm if last else None)
        if need_ctx:
            xc = ffn(xc, norm_ffn2[layer], mc[6:9], f2_in, f2_out, layer)
    return xs[None]
```
